```python
import jax, jax.numpy as jnp
from jax import lax
import numpy as np

D_MODEL = 2048
BATCH = 4
SEQ = 4096
DEPTH = 4

N_HEADS = 16
HEAD_DIM = 128
NSA_KV_HEADS = 4
CMP_BLOCK = 32
CMP_STRIDE = 16
CMP_HIDDEN = 256
SEL_BLOCK = 64
SEL_TOPK = 16
WINDOW = 512
NSA_Q_CHUNK = 32
SB_Q_BLOCK = 128
D_FF = 5632
ROPE_THETA = 10000.0
EPS = 1e-6
N_A_LAYERS = DEPTH // 2
N_B_LAYERS = DEPTH - N_A_LAYERS
NSA_IN_DIM = N_HEADS * HEAD_DIM + 6 * NSA_KV_HEADS * HEAD_DIM + 3 * N_HEADS
NEG = -1e30
FORCE = 1e9

kernel_name = "yoco_nsa_stickbreaking_hybrid"


def rms_norm(x, g):
    xf = x.astype(jnp.float32)
    y = xf * lax.rsqrt(jnp.mean(xf * xf, axis=-1, keepdims=True) + EPS)
    return (y * g.astype(jnp.float32)).astype(x.dtype)


def modulate(h, shift, scale):
    return h * (1 + scale[:, None, :]) + shift[:, None, :]


def rope(x, pos):
    half = HEAD_DIM // 2
    inv = ROPE_THETA ** (-jnp.arange(half, dtype=jnp.float32) / half)
    ang = pos.astype(jnp.float32)[:, None] * inv[None, :]
    cos = jnp.cos(ang)[None, :, None, :]
    sin = jnp.sin(ang)[None, :, None, :]
    xf = x.astype(jnp.float32)
    x1, x2 = xf[..., :half], xf[..., half:]
    return jnp.concatenate([x1 * cos - x2 * sin, x2 * cos + x1 * sin], axis=-1).astype(x.dtype)


def compress(kv, pe, w1, w2):
    B, S, G, Dh = kv.shape
    nc = (S - CMP_BLOCK) // CMP_STRIDE + 1
    idx = np.arange(nc)[:, None] * CMP_STRIDE + np.arange(CMP_BLOCK)[None, :]
    blocks = kv[:, idx] + pe[None, None, :, None, :]
    flat = blocks.transpose(0, 1, 3, 2, 4).reshape(B, nc, G, CMP_BLOCK * Dh)
    return jax.nn.gelu(flat @ w1) @ w2


def nsa_attention(q, kc, vc, ks, vs, kw, vw, gates):
    B, S, H, Dh = q.shape
    G = NSA_KV_HEADS
    R = H // G
    nc = kc.shape[1]
    nb = S // SEL_BLOCK
    n_sel = min(SEL_TOPK, nb)
    QC = NSA_Q_CHUNK
    scale = HEAD_DIM ** -0.5
    qg = q.reshape(B, S, G, R, Dh)
    gg = gates.reshape(B, S, G, R, 3)
    cmp_end = jnp.arange(nc) * CMP_STRIDE + CMP_BLOCK - 1
    c_start = np.arange(nc)[:, None] * CMP_STRIDE
    s_start = np.arange(nb)[None, :] * SEL_BLOCK
    overlap = jnp.asarray(((c_start < s_start + SEL_BLOCK) & (c_start + CMP_BLOCK > s_start)).astype(np.float32))
    blk_ids = jnp.arange(nb)
    ks_blk = ks.reshape(B, nb, SEL_BLOCK, G, Dh).transpose(0, 3, 1, 2, 4)
    vs_blk = vs.reshape(B, nb, SEL_BLOCK, G, Dh).transpose(0, 3, 1, 2, 4)
    kw_pad = jnp.pad(kw, ((0, 0), (WINDOW, 0), (0, 0), (0, 0)))
    vw_pad = jnp.pad(vw, ((0, 0), (WINDOW, 0), (0, 0), (0, 0)))
    vc32 = vc.astype(jnp.float32)
    bi = jnp.arange(B)[:, None, None, None]
    gi = jnp.arange(G)[None, :, None, None]

    def chunk(ci):
        start = ci * QC
        qc = lax.dynamic_slice_in_dim(qg, start, QC, axis=1)
        gc = lax.dynamic_slice_in_dim(gg, start, QC, axis=1).astype(jnp.float32)
        t = start + jnp.arange(QC)
        s = jnp.einsum('bqgrd,bngd->bgrqn', qc, kc).astype(jnp.float32) * scale
        valid = cmp_end[None, :] <= t[:, None]
        p_c = jax.nn.softmax(jnp.where(valid, s, NEG), axis=-1) * jnp.any(valid, axis=-1)[:, None].astype(jnp.float32)
        o_c = jnp.einsum('bgrqn,bngd->bqgrd', p_c, vc32)
        imp = jnp.einsum('bgrqn,nm->bgqm', p_c, overlap)
        cur = t // SEL_BLOCK
        forced = (blk_ids[None, :] == 0) | (blk_ids[None, :] == cur[:, None]) | (blk_ids[None, :] == cur[:, None] - 1)
        avail = blk_ids[None, :] * SEL_BLOCK <= t[:, None]
        imp = jnp.where(forced, FORCE, jnp.where(avail, imp, -FORCE))
        _, sel = lax.top_k(imp, n_sel)
        k_sel = ks_blk[bi, gi, sel]
        v_sel = vs_blk[bi, gi, sel].astype(jnp.float32)
        tok = sel[..., None] * SEL_BLOCK + jnp.arange(SEL_BLOCK)
        m_s = tok <= t[None, None, :, None, None]
        s = jnp.einsum('bqgrd,bgqnld->bgrqnl', qc, k_sel).astype(jnp.float32) * scale
        s = jnp.where(m_s[:, :, None], s, NEG).reshape(B, G, R, QC, n_sel * SEL_BLOCK)
        p_s = jax.nn.softmax(s, axis=-1).reshape(B, G, R, QC, n_sel, SEL_BLOCK)
        o_s = jnp.einsum('bgrqnl,bgqnld->bqgrd', p_s, v_sel)
        kwin = lax.dynamic_slice_in_dim(kw_pad, start, WINDOW + QC, axis=1)
        vwin = lax.dynamic_slice_in_dim(vw_pad, start, WINDOW + QC, axis=1).astype(jnp.float32)
        kpos = start - WINDOW + jnp.arange(WINDOW + QC)
        m_w = (kpos[None, :] <= t[:, None]) & (kpos[None, :] > t[:, None] - WINDOW) & (kpos[None, :] >= 0)
        s = jnp.einsum('bqgrd,bkgd->bgrqk', qc, kwin).astype(jnp.float32) * scale
        p_w = jax.nn.softmax(jnp.where(m_w, s, NEG), axis=-1)
        o_w = jnp.einsum('bgrqk,bkgd->bqgrd', p_w, vwin)
        return gc[..., 0:1] * o_c + gc[..., 1:2] * o_s + gc[..., 2:3] * o_w

    out = lax.map(chunk, jnp.arange(S // QC))
    return out.transpose(1, 0, 2, 3, 4, 5).reshape(B, S, H, Dh)


def nsa_mixer(h, pos, w_in, gate_b, cmp_pe, cmp_w1, cmp_w2, w_out):
    B, S, _ = h.shape
    proj = h @ w_in
    sizes = [N_HEADS * HEAD_DIM] + [NSA_KV_HEADS * HEAD_DIM] * 6
    q, kc, vc, ks, vs, kw, vw, g = jnp.split(proj, np.cumsum(sizes).tolist(), axis=-1)
    kv_shape = (B, S, NSA_KV_HEADS, HEAD_DIM)
    q = rope(q.reshape(B, S, N_HEADS, HEAD_DIM), pos)
    kc = rope(kc.reshape(kv_shape), pos)
    ks = rope(ks.reshape(kv_shape), pos)
    kw = rope(kw.reshape(kv_shape), pos)
    vc, vs, vw = vc.reshape(kv_shape), vs.reshape(kv_shape), vw.reshape(kv_shape)
    gates = jax.nn.sigmoid((g + gate_b).astype(jnp.float32)).reshape(B, S, N_HEADS, 3)
    kc = compress(kc, cmp_pe[0], cmp_w1[0], cmp_w2[0])
    vc = compress(vc, cmp_pe[1], cmp_w1[1], cmp_w2[1])
    o = nsa_attention(q, kc, vc, ks, vs, kw, vw, gates)
    return o.reshape(B, S, N_HEADS * HEAD_DIM).astype(h.dtype) @ w_out


def stick_breaking_attention(q, k, v):
    B, S, H, Dh = q.shape
    scale = HEAD_DIM ** -0.5
    kpos = jnp.arange(S)
    v32 = v.astype(jnp.float32)

    def block(bi):
        start = bi * SB_Q_BLOCK
        qb = lax.dynamic_slice_in_dim(q, start, SB_Q_BLOCK, axis=1)
        t = start + jnp.arange(SB_Q_BLOCK)
        z = jnp.einsum('bqhd,bkhd->bhqk', qb, k).astype(jnp.float32) * scale
        causal = kpos[None, :] < t[:, None]
        sp = jnp.where(causal, jax.nn.softplus(z), 0.0)
        after = lax.cumsum(sp, axis=3, reverse=True) - sp
        log_a = jax.nn.log_sigmoid(z) - after
        a = jnp.where(causal, jnp.exp(log_a), 0.0)
        return jnp.einsum('bhqk,bkhd->bqhd', a, v32)

    out = lax.map(block, jnp.arange(S // SB_Q_BLOCK))
    return out.transpose(1, 0, 2, 3, 4).reshape(B, S, H, Dh)


def sb_mixer(h, k, v, w_q, w_out):
    B, S, _ = h.shape
    q = (h @ w_q).reshape(B, S, N_HEADS, HEAD_DIM)
    o = stick_breaking_attention(q, k, v)
    return o.reshape(B, S, N_HEADS * HEAD_DIM).astype(h.dtype) @ w_out


def swiglu(h, w_in, w_out):
    gate, up = jnp.split(h @ w_in, 2, axis=-1)
    return (jax.nn.silu(gate) * up) @ w_out


def setup_inputs(seed: int = 0) -> dict:
    key = jax.random.key(seed)
    ks = jax.random.split(key, 19)
    D = D_MODEL
    f32 = jnp.float32

    def nrm(k, shape, fan_in, gain=1.0):
        return jax.random.normal(k, shape, f32) * (gain * fan_in ** -0.5)

    return {
        "x": jax.random.normal(ks[0], (BATCH, SEQ, D), f32),
        "c": jax.random.normal(ks[1], (BATCH, D), f32),
        "mod_w": nrm(ks[2], (DEPTH, D, 6 * D), D, 0.5),
        "mod_b": 0.02 * jax.random.normal(ks[3], (DEPTH, 6 * D), f32),
        "norm_g": 1.0 + 0.05 * jax.random.normal(ks[4], (DEPTH, 4, D), f32),
        "ffn_w_in": nrm(ks[5], (DEPTH, D, 2 * D_FF), D),
        "ffn_w_out": nrm(ks[6], (DEPTH, D_FF, D), D_FF),
        "a_w_in": nrm(ks[7], (N_A_LAYERS, D, NSA_IN_DIM), D),
        "a_gate_b": 0.02 * jax.random.normal(ks[8], (N_A_LAYERS, 3 * N_HEADS), f32),
        "a_cmp_pe": 0.5 * jax.random.normal(ks[9], (N_A_LAYERS, 2, CMP_BLOCK, HEAD_DIM), f32),
        "a_cmp_w1": nrm(ks[10], (N_A_LAYERS, 2, CMP_BLOCK * HEAD_DIM, CMP_HIDDEN), CMP_BLOCK * HEAD_DIM),
        "a_cmp_w2": nrm(ks[11], (N_A_LAYERS, 2, CMP_HIDDEN, HEAD_DIM), CMP_HIDDEN),
        "a_w_out": nrm(ks[12], (N_A_LAYERS, N_HEADS * HEAD_DIM, D), N_HEADS * HEAD_DIM),
        "b_w_q": nrm(ks[13], (N_B_LAYERS, D, N_HEADS * HEAD_DIM), D),
        "b_w_out": nrm(ks[14], (N_B_LAYERS, N_HEADS * HEAD_DIM, D), N_HEADS * HEAD_DIM),
        "kv_norm_g": 1.0 + 0.05 * jax.random.normal(ks[15], (D,), f32),
        "kv_mod_w": nrm(ks[16], (D, 2 * D), D, 0.5),
        "kv_mod_b": 0.02 * jax.random.normal(ks[17], (2 * D,), f32),
        "kv_w": nrm(ks[18], (D, 2 * N_HEADS * HEAD_DIM), D),
    }


def reference(x, c, mod_w, mod_b, norm_g, ffn_w_in, ffn_w_out, a_w_in, a_gate_b, a_cmp_pe, a_cmp_w1, a_cmp_w2, a_w_out, b_w_q, b_w_out, kv_norm_g, kv_mod_w, kv_mod_b, kv_w):
    B, S, _ = x.shape
    pos = jnp.arange(S)
    c_act = jax.nn.silu(c)
    k_sh = None
    v_sh = None
    for layer in range(DEPTH):
        mod = c_act @ mod_w[layer] + mod_b[layer]
        sh1, sc1, g1, sh2, sc2, g2 = jnp.split(mod, 6, axis=-1)
        h = modulate(rms_norm(x, norm_g[layer, 0]), sh1, sc1)
        if layer < N_A_LAYERS:
            y = nsa_mixer(h, pos, a_w_in[layer], a_gate_b[layer], a_cmp_pe[layer], a_cmp_w1[layer], a_cmp_w2[layer], a_w_out[layer])
        else:
            if layer == N_A_LAYERS:
                kv_sh, kv_sc = jnp.split(c_act @ kv_mod_w + kv_mod_b, 2, axis=-1)
                hk = modulate(rms_norm(x, kv_norm_g), kv_sh, kv_sc)
                k_sh, v_sh = jnp.split(hk @ kv_w, 2, axis=-1)
                k_sh = k_sh.reshape(B, S, N_HEADS, HEAD_DIM)
                v_sh = v_sh.reshape(B, S, N_HEADS, HEAD_DIM)
            j = layer - N_A_LAYERS
            y = sb_mixer(h, k_sh, v_sh, b_w_q[j], b_w_out[j])
        x = x + g1[:, None, :] * rms_norm(y, norm_g[layer, 1])
        h = modulate(rms_norm(x, norm_g[layer, 2]), sh2, sc2)
        y = swiglu(h, ffn_w_in[layer], ffn_w_out[layer])
        x = x + g2[:, None, :] * rms_norm(y, norm_g[layer, 3])
    return x
```

```python
import functools

import jax
import jax.numpy as jnp
import numpy as np
from jax import lax
from jax.experimental import pallas as pl
from jax.experimental.pallas import tpu as pltpu

N_HEADS = 16
HEAD_DIM = 128
NSA_KV_HEADS = 4
HEADS_PER_GROUP = N_HEADS // NSA_KV_HEADS
CMP_BLOCK = 32
CMP_STRIDE = 16
CMP_HIDDEN = 256
SEL_BLOCK = 64
SEL_TOPK = 16
WINDOW = 512
D_FF = 5632
ROPE_THETA = 10000.0
EPS = 1e-6
NEG = -1e30
FORCE = 1e9
ATTN_SCALE = HEAD_DIM ** -0.5
UNSELECTED = -(2.0 ** 100)

LANES = 128
BF16 = jnp.bfloat16
F32 = jnp.float32

ROW_TILE = 1024
OUT_ROW_TILE = 512
PROJ_COL_TILE = 512
FF_TILE = 512
NSA_Q_TILE = 128
NSA_K_TILE = 512
SB_TILE = 256
MOD_COL_TILE = 1024
VMEM_LIMIT = 56 * 1024 * 1024

_NT = (((1,), (1,)), ((), ()))


def _params(n_axes):
    return pltpu.CompilerParams(
        dimension_semantics=("arbitrary",) * n_axes, vmem_limit_bytes=VMEM_LIMIT)


def _dot(a, b):
    return jnp.dot(a, b, preferred_element_type=F32)


def _dot_nt(a, b):
    return lax.dot_general(a, b, _NT, preferred_element_type=F32)


def _split_bf16(x):
    hi = x.astype(BF16)
    lo = (x - hi.astype(F32)).astype(BF16)
    return hi, lo


def _rms(x, gamma):
    ms = jnp.mean(x * x, axis=-1, keepdims=True)
    return x * lax.rsqrt(ms + EPS) * gamma


def _mod_kernel(c_ref, w_ref, b_ref, o_ref):
    c = c_ref[...]
    ca = c * jax.nn.sigmoid(c)
    c_hi, c_lo = _split_bf16(ca)
    w_hi, w_lo = _split_bf16(w_ref[...])
    acc = _dot(c_hi, w_hi) + _dot(c_lo, w_hi) + _dot(c_hi, w_lo)
    o_ref[...] = acc + b_ref[...]


def _modulation(c_pad, w, b):
    n_layers, d, n = w.shape
    tn = MOD_COL_TILE
    return pl.pallas_call(
        _mod_kernel,
        grid=(n_layers, n // tn),
        in_specs=[
            pl.BlockSpec((8, d), lambda l, j: (0, 0)),
            pl.BlockSpec((None, d, tn), lambda l, j: (l, 0, j)),
            pl.BlockSpec((None, 1, tn), lambda l, j: (l, 0, j)),
        ],
        out_specs=pl.BlockSpec((None, 8, tn), lambda l, j: (l, 0, j)),
        out_shape=jax.ShapeDtypeStruct((n_layers, 8, n), F32),
        compiler_params=_params(2),
        name="modulation",
    )(c_pad, w, b)


def _normmod(x_ref, g_ref, sh_ref, sc_ref):
    y = _rms(x_ref[...], g_ref[...])
    return y * (1.0 + sc_ref[...]) + sh_ref[...]


def _proj_kernel(x_ref, g_ref, sh_ref, sc_ref, w_ref, *rest, mode, out_scale):
    if mode == "nsa":
        cos_ref, sin_ref, o_ref, h_ref = rest
    elif mode == "gate":
        b_ref, o_ref, h_ref = rest
    else:
        o_ref, h_ref = rest
    j = pl.program_id(1)

    @pl.when(j == 0)
    def _():
        h_ref[...] = _normmod(x_ref, g_ref, sh_ref, sc_ref).astype(BF16)

    acc = _dot(h_ref[...], w_ref[...])
    if mode == "plain":
        o_ref[...] = (acc * out_scale).astype(o_ref.dtype)
    elif mode == "gate":
        o_ref[...] = jax.nn.sigmoid(acc + b_ref[...])
    else:
        is_q = j < 4
        is_rope = is_q | (j == 4) | (j == 6) | (j == 8)

        @pl.when(is_rope)
        def _():
            scale = jnp.where(is_q, ATTN_SCALE, 1.0).astype(F32)
            cos = cos_ref[...]
            sin = sin_ref[...]
            for hh in range(PROJ_COL_TILE // HEAD_DIM):
                xh = acc[:, hh * HEAD_DIM:(hh + 1) * HEAD_DIM] * scale
                rot = xh * cos + pltpu.roll(xh, HEAD_DIM // 2, axis=1) * sin
                o_ref[:, hh * HEAD_DIM:(hh + 1) * HEAD_DIM] = rot.astype(o_ref.dtype)

        @pl.when(jnp.logical_not(is_rope))
        def _():
            o_ref[...] = acc.astype(o_ref.dtype)


def _project(x2d, seq, gamma, shift, scale, w, *, mode="plain", out_scale=1.0,
             out_dtype=BF16, extra=(), tn=PROJ_COL_TILE):
    t, d = x2d.shape
    n = w.shape[1]
    tm = min(ROW_TILE, seq)
    tiles_per_seq = seq // tm
    in_specs = [
        pl.BlockSpec((tm, d), lambda i, j: (i, 0)),
        pl.BlockSpec((1, d), lambda i, j: (0, 0)),
        pl.BlockSpec((None, 1, d), lambda i, j: (i // tiles_per_seq, 0, 0)),
        pl.BlockSpec((None, 1, d), lambda i, j: (i // tiles_per_seq, 0, 0)),
        pl.BlockSpec((d, tn), lambda i, j: (0, j)),
    ]
    if mode == "nsa":
        in_specs += [pl.BlockSpec((tm, HEAD_DIM), lambda i, j: (i % tiles_per_seq, 0))] * 2
    elif mode == "gate":
        in_specs += [pl.BlockSpec((1, tn), lambda i, j: (0, j))]
    return pl.pallas_call(
        functools.partial(_proj_kernel, mode=mode, out_scale=out_scale),
        grid=(t // tm, n // tn),
        in_specs=in_specs,
        out_specs=pl.BlockSpec((tm, tn), lambda i, j: (i, j)),
        out_shape=jax.ShapeDtypeStruct((t, n), out_dtype),
        scratch_shapes=[pltpu.VMEM((tm, d), BF16)],
        compiler_params=_params(2),
        name="proj_" + mode,
    )(x2d, gamma, shift, scale, w, *extra)


def _outproj_kernel(o_ref, w_ref, x_ref, gate_ref, g_ref, out_ref):
    y = _dot(o_ref[...], w_ref[...])
    out_ref[...] = x_ref[...] + gate_ref[...] * _rms(y, g_ref[...])


def _out_project(o2d, w, x2d, seq, gate, gamma):
    t, d = x2d.shape
    k = o2d.shape[1]
    tm = min(OUT_ROW_TILE, seq)
    tiles_per_seq = seq // tm
    return pl.pallas_call(
        _outproj_kernel,
        grid=(t // tm,),
        in_specs=[
            pl.BlockSpec((tm, k), lambda i: (i, 0)),
            pl.BlockSpec((k, d), lambda i: (0, 0)),
            pl.BlockSpec((tm, d), lambda i: (i, 0)),
            pl.BlockSpec((None, 1, d), lambda i: (i // tiles_per_seq, 0, 0)),
            pl.BlockSpec((1, d), lambda i: (0, 0)),
        ],
        out_specs=pl.BlockSpec((tm, d), lambda i: (i, 0)),
        out_shape=jax.ShapeDtypeStruct((t, d), F32),
        compiler_params=_params(1),
        name="out_proj",
    )(o2d, w, x2d, gate, gamma)


def _ffn_kernel(x_ref, g_in_ref, sh_ref, sc_ref, wg_ref, wu_ref, wo_ref, gate_ref, g_out_ref,
                out_ref, h_ref, acc_ref):
    f = pl.program_id(1)

    @pl.when(f == 0)
    def _():
        h_ref[...] = _normmod(x_ref, g_in_ref, sh_ref, sc_ref).astype(BF16)
        acc_ref[...] = jnp.zeros_like(acc_ref)

    h = h_ref[...]
    gate = _dot(h, wg_ref[...])
    up = _dot(h, wu_ref[...])
    act = (gate * jax.nn.sigmoid(gate) * up).astype(BF16)
    acc_ref[...] += _dot(act, wo_ref[...])

    @pl.when(f == pl.num_programs(1) - 1)
    def _():
        out_ref[...] = x_ref[...] + gate_ref[...] * _rms(acc_ref[...], g_out_ref[...])


def _ffn(x2d, seq, g_in, shift, scale, w_in, w_out, gate, g_out):
    t, d = x2d.shape
    d_ff = w_out.shape[0]
    tm = min(OUT_ROW_TILE, seq)
    tf = FF_TILE
    n_f = d_ff // tf
    tiles_per_seq = seq // tm
    batch_vec = pl.BlockSpec((None, 1, d), lambda i, f: (i // tiles_per_seq, 0, 0))
    vec = pl.BlockSpec((1, d), lambda i, f: (0, 0))
    return pl.pallas_call(
        _ffn_kernel,
        grid=(t // tm, n_f),
        in_specs=[
            pl.BlockSpec((tm, d), lambda i, f: (i, 0)),
            vec, batch_vec, batch_vec,
            pl.BlockSpec((d, tf), lambda i, f: (0, f)),
            pl.BlockSpec((d, tf), lambda i, f: (0, n_f + f)),
            pl.BlockSpec((tf, d), lambda i, f: (f, 0)),
            batch_vec, vec,
        ],
        out_specs=pl.BlockSpec((tm, d), lambda i, f: (i, 0)),
        out_shape=jax.ShapeDtypeStruct((t, d), F32),
        scratch_shapes=[pltpu.VMEM((tm, d), BF16), pltpu.VMEM((tm, d), F32)],
        compiler_params=_params(2),
        name="ffn",
    )(x2d, g_in, shift, scale, w_in, w_in, w_out, gate, g_out)


def _gelu_tanh(x):
    return 0.5 * x * (1.0 + jnp.tanh(np.sqrt(2.0 / np.pi) * (x + 0.044715 * (x * x * x))))


def _compress_kernel(c_ref, w1_ref, w2_ref, pe_ref, o_ref):
    half = CMP_STRIDE * HEAD_DIM
    c = c_ref[...]
    n_chunks = c.shape[0]
    first = _dot(c, w1_ref[0:half, :])
    second = _dot(c, w1_ref[half:2 * half, :])
    pe = jnp.broadcast_to(pe_ref[...], (8, 2 * half)).astype(BF16)
    bias = _dot(pe, w1_ref[...])[0:1, :]
    hidden = first + pltpu.roll(second, n_chunks - 1, axis=0) + bias
    out = _dot(_gelu_tanh(hidden).astype(BF16), w2_ref[...])
    row = lax.broadcasted_iota(jnp.int32, out.shape, 0)
    o_ref[...] = jnp.where(row < n_chunks - 1, out, 0.0).astype(o_ref.dtype)


def _compress(chunks, w1, w2, pe):
    _, b, g, n_chunks, width = chunks.shape
    return pl.pallas_call(
        _compress_kernel,
        grid=(2, b, g),
        in_specs=[
            pl.BlockSpec((None, None, None, n_chunks, width), lambda w, bi, gi: (w, bi, gi, 0, 0)),
            pl.BlockSpec((None, 2 * width, CMP_HIDDEN), lambda w, bi, gi: (w, 0, 0)),
            pl.BlockSpec((None, CMP_HIDDEN, HEAD_DIM), lambda w, bi, gi: (w, 0, 0)),
            pl.BlockSpec((None, 1, 2 * width), lambda w, bi, gi: (w, 0, 0)),
        ],
        out_specs=pl.BlockSpec((None, None, None, n_chunks, HEAD_DIM),
                               lambda w, bi, gi: (w, bi, gi, 0, 0)),
        out_shape=jax.ShapeDtypeStruct((2, b, g, n_chunks, HEAD_DIM), BF16),
        compiler_params=_params(3),
        name="compress",
    )(chunks, w1, w2, pe)


def _nsa_kernel(q_ref, kc_ref, vc_ref, ks_ref, vs_ref, kw_ref, vw_ref, gate_ref, o_ref,
                kaug_ref, qaug_ref, m_ref, l_ref, acc_ref):
    tq = NSA_Q_TILE
    tk = NSA_K_TILE
    r_heads = HEADS_PER_GROUP
    rows = r_heads * tq
    seq = ks_ref.shape[0]
    n_cmp = kc_ref.shape[0]
    qi = pl.program_id(2)
    t0 = qi * tq

    @pl.when(qi == 0)
    def _():
        kaug_ref[:, 0:HEAD_DIM] = ks_ref[...]
        row = lax.broadcasted_iota(jnp.int32, (seq, LANES), 0)
        lane = lax.broadcasted_iota(jnp.int32, (seq, LANES), 1)
        kaug_ref[:, HEAD_DIM:2 * HEAD_DIM] = jnp.where(
            (row // SEL_BLOCK) == lane, 1.0, 0.0).astype(BF16)

    t_c = t0 + lax.broadcasted_iota(jnp.int32, (tq, n_cmp), 0)
    n_idx = lax.broadcasted_iota(jnp.int32, (tq, n_cmp), 1)
    valid = (n_idx * CMP_STRIDE + (CMP_BLOCK - 1)) <= t_c
    t_1 = t0 + lax.broadcasted_iota(jnp.int32, (tq, 1), 0)
    any_valid = jnp.where(t_1 >= CMP_BLOCK - 1, 1.0, 0.0).astype(F32)
    kc = kc_ref[...]
    vc = vc_ref[...]
    p_sum = jnp.zeros((tq, n_cmp), F32)
    o_cmp = []
    for r in range(r_heads):
        qr = q_ref[:, r * HEAD_DIM:(r + 1) * HEAD_DIM]
        s = jnp.where(valid, _dot_nt(qr, kc), NEG)
        e = jnp.exp(s - jnp.max(s, axis=-1, keepdims=True))
        p = e / jnp.sum(e, axis=-1, keepdims=True) * any_valid
        p_sum = p_sum + p
        o_cmp.append(_dot(p.astype(BF16), vc))

    n_o = lax.broadcasted_iota(jnp.int32, (n_cmp, LANES), 0)
    m_o = lax.broadcasted_iota(jnp.int32, (n_cmp, LANES), 1)
    overlap = jnp.where(n_o * CMP_STRIDE < (m_o + 1) * SEL_BLOCK,
                        jnp.where(n_o * CMP_STRIDE + CMP_BLOCK > m_o * SEL_BLOCK, 1.0, 0.0),
                        0.0).astype(BF16)
    p_hi, p_lo = _split_bf16(p_sum)
    imp = _dot(p_hi, overlap) + _dot(p_lo, overlap)
    lane = lax.broadcasted_iota(jnp.int32, (tq, LANES), 1)
    t_s = t0 + lax.broadcasted_iota(jnp.int32, (tq, LANES), 0)
    cur = t_s // SEL_BLOCK
    forced = (lane == 0) | (lane == cur) | (lane == cur - 1)
    avail = lane * SEL_BLOCK <= t_s
    imp = jnp.where(forced, FORCE, jnp.where(avail, imp, -FORCE))
    rank = jnp.zeros((tq, LANES), F32)
    n_blocks = seq // SEL_BLOCK
    for mp in range(n_blocks):
        col = imp[:, mp:mp + 1]
        ahead = jnp.where(col > imp, 1.0, jnp.where(col == imp, jnp.where(lane > mp, 1.0, 0.0), 0.0))
        rank = rank + ahead
    n_sel = min(SEL_TOPK, n_blocks)
    mask_feat = jnp.where(rank < n_sel, 0.0, UNSELECTED).astype(BF16)

    for r in range(r_heads):
        qaug_ref[r * tq:(r + 1) * tq, 0:HEAD_DIM] = q_ref[:, r * HEAD_DIM:(r + 1) * HEAD_DIM]
        qaug_ref[r * tq:(r + 1) * tq, HEAD_DIM:2 * HEAD_DIM] = mask_feat

    t_row = t0 + (lax.broadcasted_iota(jnp.int32, (rows, 1), 0) & (tq - 1))

    m_ref[...] = jnp.full_like(m_ref, NEG)
    l_ref[...] = jnp.zeros_like(l_ref)
    acc_ref[...] = jnp.zeros_like(acc_ref)

    def sel_step(j, causal):
        start = pl.multiple_of(j * tk, tk)
        s = _dot_nt(qaug_ref[...], kaug_ref[pl.ds(start, tk), :])
        if causal:
            kpos = start + lax.broadcasted_iota(jnp.int32, (rows, tk), 1)
            s = jnp.where(kpos <= t_row, s, NEG)
        m_old = m_ref[...]
        m_new = jnp.maximum(m_old, jnp.max(s, axis=-1, keepdims=True))
        alpha = jnp.exp(m_old - m_new)
        p = jnp.exp(s - m_new[:, 0:1])
        l_ref[...] = alpha * l_ref[...] + jnp.sum(p, axis=-1, keepdims=True)
        acc_ref[...] = alpha * acc_ref[...] + _dot(p.astype(BF16), vs_ref[pl.ds(start, tk), :])
        m_ref[...] = m_new

    n_full = t0 // tk

    def full_body(j, carry):
        sel_step(j, False)
        return carry

    lax.fori_loop(0, n_full, full_body, 0)
    sel_step(n_full, True)

    span = WINDOW + tq
    w_start = pl.multiple_of(jnp.maximum(t0 - WINDOW, 0), tq)
    s = _dot_nt(qaug_ref[:, 0:HEAD_DIM], kw_ref[pl.ds(w_start, span), :])
    kpos = w_start + lax.broadcasted_iota(jnp.int32, (rows, span), 1)
    in_window = (kpos <= t_row) & (kpos > t_row - WINDOW)
    s = jnp.where(in_window, s, NEG)
    e = jnp.exp(s - jnp.max(s, axis=-1, keepdims=True))
    o_win = _dot(e.astype(BF16), vw_ref[pl.ds(w_start, span), :]) / jnp.sum(e, axis=-1, keepdims=True)

    o_sel = acc_ref[...] / l_ref[...]
    gates = gate_ref[...]
    for r in range(r_heads):
        rs = slice(r * tq, (r + 1) * tq)
        out = (gates[:, 3 * r:3 * r + 1] * o_cmp[r]
               + gates[:, 3 * r + 1:3 * r + 2] * o_sel[rs]
               + gates[:, 3 * r + 2:3 * r + 3] * o_win[rs])
        o_ref[:, r * HEAD_DIM:(r + 1) * HEAD_DIM] = out.astype(o_ref.dtype)


def _nsa_attention(proj, comp, gates, batch, seq):
    t = proj.shape[0]
    tq = NSA_Q_TILE
    n_q = seq // tq
    rows = HEADS_PER_GROUP * tq
    n_chunks = comp.shape[3]
    kv_cols = NSA_KV_HEADS
    q_cols = N_HEADS

    def kv_spec(which):
        base = q_cols + which * kv_cols
        return pl.BlockSpec((seq, HEAD_DIM), lambda b, g, qi: (b, base + g))

    def cmp_spec(which):
        return pl.BlockSpec((None, None, None, n_chunks, HEAD_DIM),
                            lambda b, g, qi: (which, b, g, 0, 0))

    return pl.pallas_call(
        _nsa_kernel,
        grid=(batch, NSA_KV_HEADS, n_q),
        in_specs=[
            pl.BlockSpec((tq, HEADS_PER_GROUP * HEAD_DIM), lambda b, g, qi: (b * n_q + qi, g)),
            cmp_spec(0), cmp_spec(1),
            kv_spec(2), kv_spec(3), kv_spec(4), kv_spec(5),
            pl.BlockSpec((tq, LANES), lambda b, g, qi: (b * n_q + qi, g)),
        ],
        out_specs=pl.BlockSpec((tq, HEADS_PER_GROUP * HEAD_DIM), lambda b, g, qi: (b * n_q + qi, g)),
        out_shape=jax.ShapeDtypeStruct((t, N_HEADS * HEAD_DIM), BF16),
        scratch_shapes=[
            pltpu.VMEM((seq, 2 * HEAD_DIM), BF16),
            pltpu.VMEM((rows, 2 * HEAD_DIM), BF16),
            pltpu.VMEM((rows, LANES), F32),
            pltpu.VMEM((rows, LANES), F32),
            pltpu.VMEM((rows, HEAD_DIM), F32),
        ],
        compiler_params=_params(3),
        name="nsa_attention",
    )(proj, comp, comp, proj, proj, proj, proj, gates)


def _sb_kernel(q_ref, k_ref, v_ref, o_ref, carry_ref, acc_ref):
    tq = SB_TILE
    tk = SB_TILE
    qi = pl.program_id(2)
    q = q_ref[...]
    row = lax.broadcasted_iota(jnp.int32, (tk, tk), 0)
    col = lax.broadcasted_iota(jnp.int32, (tk, tk), 1)
    later = jnp.where(row > col, 1.0, 0.0).astype(BF16)
    carry_ref[...] = jnp.zeros_like(carry_ref)
    acc_ref[...] = jnp.zeros_like(acc_ref)

    def step(j, diagonal):
        start = pl.multiple_of(j * tk, tk)
        z = _dot_nt(q, k_ref[pl.ds(start, tk), :])
        sp = jnp.maximum(z, 0.0) + jnp.log(1.0 + jnp.exp(-jnp.abs(z)))
        log_beta = z - sp
        if diagonal:
            causal = col < row
            sp = jnp.where(causal, sp, 0.0)
        sp_hi, sp_lo = _split_bf16(sp)
        after = _dot(sp_hi, later) + _dot(sp_lo, later)
        carry = carry_ref[...]
        log_a = log_beta - after - jnp.concatenate([carry] * (tk // LANES), axis=1)
        a = jnp.exp(log_a)
        if diagonal:
            a = jnp.where(causal, a, 0.0)
        acc_ref[...] += _dot(a.astype(BF16), v_ref[pl.ds(start, tk), :])
        total = after[:, 0:1] + sp[:, 0:1]
        carry_ref[...] = carry + total

    step(qi, True)

    def body(i, c):
        step(qi - 1 - i, False)
        return c

    lax.fori_loop(0, qi, body, 0)
    o_ref[...] = acc_ref[...].astype(o_ref.dtype)


def _sb_attention(q, kv, batch, seq):
    t = q.shape[0]
    tq = SB_TILE
    n_q = seq // tq
    return pl.pallas_call(
        _sb_kernel,
        grid=(batch, N_HEADS, n_q),
        in_specs=[
            pl.BlockSpec((tq, HEAD_DIM), lambda b, h, qi: (b * n_q + qi, h)),
            pl.BlockSpec((seq, HEAD_DIM), lambda b, h, qi: (b, h)),
            pl.BlockSpec((seq, HEAD_DIM), lambda b, h, qi: (b, N_HEADS + h)),
        ],
        out_specs=pl.BlockSpec((tq, HEAD_DIM), lambda b, h, qi: (b * n_q + qi, h)),
        out_shape=jax.ShapeDtypeStruct((t, N_HEADS * HEAD_DIM), BF16),
        scratch_shapes=[pltpu.VMEM((tq, LANES), F32), pltpu.VMEM((tq, HEAD_DIM), F32)],
        compiler_params=_params(3),
        name="sb_attention",
    )(q, kv, kv)


def _rope_tables(seq):
    half = HEAD_DIM // 2
    inv = ROPE_THETA ** (-jnp.arange(half, dtype=F32) / half)
    ang = jnp.arange(seq, dtype=F32)[:, None] * inv[None, :]
    cos, sin = jnp.cos(ang), jnp.sin(ang)
    return jnp.concatenate([cos, cos], axis=-1), jnp.concatenate([-sin, sin], axis=-1)


def kernel(x, c, mod_w, mod_b, norm_g, ffn_w_in, ffn_w_out, a_w_in, a_gate_b, a_cmp_pe, a_cmp_w1,
           a_cmp_w2, a_w_out, b_w_q, b_w_out, kv_norm_g, kv_mod_w, kv_mod_b, kv_w):
    batch, seq, d = x.shape
    depth = mod_w.shape[0]
    n_a = a_w_in.shape[0]
    t = batch * seq
    hd = N_HEADS * HEAD_DIM
    kvd = NSA_KV_HEADS * HEAD_DIM
    x2d = x.reshape(t, d)

    c_pad = jnp.pad(c, ((0, 8 - batch), (0, 0)))
    mod = _modulation(c_pad, mod_w, mod_b[:, None, :])[:, :batch]
    mod = mod.reshape(depth, batch, 6, 1, d)
    kv_mod = _modulation(c_pad, kv_mod_w[None], kv_mod_b[None, None, :])[0, :batch]
    kv_mod = kv_mod.reshape(batch, 2, 1, d)
    cos, sin = _rope_tables(seq)

    kv = None
    for layer in range(depth):
        sh1, sc1, g1, sh2, sc2, g2 = (mod[layer, :, i] for i in range(6))
        gam = norm_g[layer][:, None, :]
        if layer < n_a:
            w_in = a_w_in[layer]
            n_main = hd + 6 * kvd
            proj = _project(x2d, seq, gam[0], sh1, sc1, w_in[:, :n_main].astype(BF16),
                            mode="nsa", extra=(cos, sin))
            n_gate = 3 * HEADS_PER_GROUP
            w_gate = w_in[:, n_main:].reshape(d, NSA_KV_HEADS, n_gate)
            w_gate = jnp.pad(w_gate, ((0, 0), (0, 0), (0, LANES - n_gate))).reshape(d, NSA_KV_HEADS * LANES)
            b_gate = jnp.pad(a_gate_b[layer].reshape(NSA_KV_HEADS, n_gate),
                             ((0, 0), (0, LANES - n_gate))).reshape(1, NSA_KV_HEADS * LANES)
            gates = _project(x2d, seq, gam[0], sh1, sc1, w_gate.astype(BF16), mode="gate",
                             out_dtype=F32, extra=(b_gate,), tn=NSA_KV_HEADS * LANES)
            n_chunks = seq // CMP_STRIDE
            chunks = proj[:, hd:hd + 2 * kvd].reshape(
                batch, n_chunks, CMP_STRIDE, 2, NSA_KV_HEADS, HEAD_DIM)
            chunks = chunks.transpose(3, 0, 4, 1, 2, 5).reshape(
                2, batch, NSA_KV_HEADS, n_chunks, CMP_STRIDE * HEAD_DIM)
            comp = _compress(chunks, a_cmp_w1[layer].astype(BF16), a_cmp_w2[layer].astype(BF16),
                             a_cmp_pe[layer].reshape(2, 1, CMP_BLOCK * HEAD_DIM))
            o = _nsa_attention(proj, comp, gates, batch, seq)
            x2d = _out_project(o, a_w_out[layer].astype(BF16), x2d, seq, g1, gam[1])
        else:
            jb = layer - n_a
            if kv is None:
                kv = _project(x2d, seq, kv_norm_g[None, :], kv_mod[:, 0], kv_mod[:, 1],
                              kv_w.astype(BF16))
            q = _project(x2d, seq, gam[0], sh1, sc1, b_w_q[jb].astype(BF16), out_scale=ATTN_SCALE)
            o = _sb_attention(q, kv, batch, seq)
            x2d = _out_project(o, b_w_out[jb].astype(BF16), x2d, seq, g1, gam[1])
        x2d = _ffn(x2d, seq, gam[2], sh2, sc2, ffn_w_in[layer].astype(BF16),
                   ffn_w_out[layer].astype(BF16), g2, gam[3])
    return x2d.reshape(batch, seq, d)
```

```python
import functools

import jax
import jax.numpy as jnp
import numpy as np
from jax import lax
from jax.experimental import pallas as pl
from jax.experimental.pallas import tpu as pltpu

N_HEADS = 16
HEAD_DIM = 128
NSA_KV_HEADS = 4
HEADS_PER_GROUP = N_HEADS // NSA_KV_HEADS
CMP_BLOCK = 32
CMP_STRIDE = 16
CMP_HIDDEN = 256
SEL_BLOCK = 64
SEL_TOPK = 16
WINDOW = 512
D_FF = 5632
ROPE_THETA = 10000.0
EPS = 1e-6
NEG = -1e30
FORCE = 1e9
ATTN_SCALE = HEAD_DIM ** -0.5
UNSELECTED = -(2.0 ** 100)

LANES = 128
BF16 = jnp.bfloat16
F32 = jnp.float32

ROW_TILE = 1024
OUT_ROW_TILE = 512
PROJ_COL_TILE = 512
FF_TILE = 512
NSA_Q_TILE = 128
NSA_K_TILE = 512
SB_TILE = 256
SB_HEADS = 4
MOD_COL_TILE = 1024
VMEM_LIMIT = 56 * 1024 * 1024

_NT = (((1,), (1,)), ((), ()))


def _params(n_axes):
    return pltpu.CompilerParams(
        dimension_semantics=("arbitrary",) * n_axes, vmem_limit_bytes=VMEM_LIMIT)


def _dot(a, b):
    return jnp.dot(a, b, preferred_element_type=F32)


def _dot_nt(a, b):
    return lax.dot_general(a, b, _NT, preferred_element_type=F32)


def _split_bf16(x):
    hi = x.astype(BF16)
    lo = (x - hi.astype(F32)).astype(BF16)
    return hi, lo


def _rms(x, gamma):
    ms = jnp.mean(x * x, axis=-1, keepdims=True)
    return x * lax.rsqrt(ms + EPS) * gamma


def _mod_kernel(c_ref, w_ref, b_ref, o_ref):
    c = c_ref[...]
    ca = c * jax.nn.sigmoid(c)
    c_hi, c_lo = _split_bf16(ca)
    w_hi, w_lo = _split_bf16(w_ref[...])
    acc = _dot(c_hi, w_hi) + _dot(c_lo, w_hi) + _dot(c_hi, w_lo)
    o_ref[...] = acc + b_ref[...]


def _modulation(c_pad, w, b):
    n_layers, d, n = w.shape
    tn = MOD_COL_TILE
    return pl.pallas_call(
        _mod_kernel,
        grid=(n_layers, n // tn),
        in_specs=[
            pl.BlockSpec((8, d), lambda l, j: (0, 0)),
            pl.BlockSpec((None, d, tn), lambda l, j: (l, 0, j)),
            pl.BlockSpec((None, 1, tn), lambda l, j: (l, 0, j)),
        ],
        out_specs=pl.BlockSpec((None, 8, tn), lambda l, j: (l, 0, j)),
        out_shape=jax.ShapeDtypeStruct((n_layers, 8, n), F32),
        compiler_params=_params(2),
        name="modulation",
    )(c_pad, w, b)


def _normmod(x_ref, g_ref, sh_ref, sc_ref):
    y = _rms(x_ref[...], g_ref[...])
    return y * (1.0 + sc_ref[...]) + sh_ref[...]


def _proj_kernel(x_ref, g_ref, sh_ref, sc_ref, w_ref, *rest, mode, out_scale):
    if mode == "nsa":
        cos_ref, sin_ref, o_ref, h_ref = rest
    elif mode == "gate":
        b_ref, o_ref, h_ref = rest
    else:
        o_ref, h_ref = rest
    j = pl.program_id(1)

    @pl.when(j == 0)
    def _():
        h_ref[...] = _normmod(x_ref, g_ref, sh_ref, sc_ref).astype(BF16)

    acc = _dot(h_ref[...], w_ref[...])
    if mode == "plain":
        o_ref[...] = (acc * out_scale).astype(o_ref.dtype)
    elif mode == "gate":
        o_ref[...] = jax.nn.sigmoid(acc + b_ref[...])
    else:
        is_q = j < 4
        is_rope = is_q | (j == 4) | (j == 6) | (j == 8)

        @pl.when(is_rope)
        def _():
            scale = jnp.where(is_q, ATTN_SCALE, 1.0).astype(F32)
            cos = cos_ref[...]
            sin = sin_ref[...]
            for hh in range(PROJ_COL_TILE // HEAD_DIM):
                xh = acc[:, hh * HEAD_DIM:(hh + 1) * HEAD_DIM] * scale
                rot = xh * cos + pltpu.roll(xh, HEAD_DIM // 2, axis=1) * sin
                o_ref[:, hh * HEAD_DIM:(hh + 1) * HEAD_DIM] = rot.astype(o_ref.dtype)

        @pl.when(jnp.logical_not(is_rope))
        def _():
            o_ref[...] = acc.astype(o_ref.dtype)


def _project(x2d, seq, gamma, shift, scale, w, *, mode="plain", out_scale=1.0,
             out_dtype=BF16, extra=(), tn=PROJ_COL_TILE):
    t, d = x2d.shape
    n = w.shape[1]
    tm = min(ROW_TILE, seq)
    tiles_per_seq = seq // tm
    in_specs = [
        pl.BlockSpec((tm, d), lambda i, j: (i, 0)),
        pl.BlockSpec((1, d), lambda i, j: (0, 0)),
        pl.BlockSpec((None, 1, d), lambda i, j: (i // tiles_per_seq, 0, 0)),
        pl.BlockSpec((None, 1, d), lambda i, j: (i // tiles_per_seq, 0, 0)),
        pl.BlockSpec((d, tn), lambda i, j: (0, j)),
    ]
    if mode == "nsa":
        in_specs += [pl.BlockSpec((tm, HEAD_DIM), lambda i, j: (i % tiles_per_seq, 0))] * 2
    elif mode == "gate":
        in_specs += [pl.BlockSpec((1, tn), lambda i, j: (0, j))]
    return pl.pallas_call(
        functools.partial(_proj_kernel, mode=mode, out_scale=out_scale),
        grid=(t // tm, n // tn),
        in_specs=in_specs,
        out_specs=pl.BlockSpec((tm, tn), lambda i, j: (i, j)),
        out_shape=jax.ShapeDtypeStruct((t, n), out_dtype),
        scratch_shapes=[pltpu.VMEM((tm, d), BF16)],
        compiler_params=_params(2),
        name="proj_" + mode,
    )(x2d, gamma, shift, scale, w, *extra)


def _outproj_kernel(o_ref, w_ref, x_ref, gate_ref, g_ref, out_ref):
    y = _dot(o_ref[...], w_ref[...])
    out_ref[...] = x_ref[...] + gate_ref[...] * _rms(y, g_ref[...])


def _out_project(o2d, w, x2d, seq, gate, gamma):
    t, d = x2d.shape
    k = o2d.shape[1]
    tm = min(OUT_ROW_TILE, seq)
    tiles_per_seq = seq // tm
    return pl.pallas_call(
        _outproj_kernel,
        grid=(t // tm,),
        in_specs=[
            pl.BlockSpec((tm, k), lambda i: (i, 0)),
            pl.BlockSpec((k, d), lambda i: (0, 0)),
            pl.BlockSpec((tm, d), lambda i: (i, 0)),
            pl.BlockSpec((None, 1, d), lambda i: (i // tiles_per_seq, 0, 0)),
            pl.BlockSpec((1, d), lambda i: (0, 0)),
        ],
        out_specs=pl.BlockSpec((tm, d), lambda i: (i, 0)),
        out_shape=jax.ShapeDtypeStruct((t, d), F32),
        compiler_params=_params(1),
        name="out_proj",
    )(o2d, w, x2d, gate, gamma)


def _ffn_kernel(x_ref, g_in_ref, sh_ref, sc_ref, wg_ref, wu_ref, wo_ref, gate_ref, g_out_ref,
                out_ref, h_ref, acc_ref):
    f = pl.program_id(1)

    @pl.when(f == 0)
    def _():
        h_ref[...] = _normmod(x_ref, g_in_ref, sh_ref, sc_ref).astype(BF16)
        acc_ref[...] = jnp.zeros_like(acc_ref)

    h = h_ref[...]
    gate = _dot(h, wg_ref[...])
    up = _dot(h, wu_ref[...])
    act = (gate * jax.nn.sigmoid(gate) * up).astype(BF16)
    acc_ref[...] += _dot(act, wo_ref[...])

    @pl.when(f == pl.num_programs(1) - 1)
    def _():
        out_ref[...] = x_ref[...] + gate_ref[...] * _rms(acc_ref[...], g_out_ref[...])


def _ffn(x2d, seq, g_in, shift, scale, w_in, w_out, gate, g_out):
    t, d = x2d.shape
    d_ff = w_out.shape[0]
    tm = min(OUT_ROW_TILE, seq)
    tf = FF_TILE
    n_f = d_ff // tf
    tiles_per_seq = seq // tm
    batch_vec = pl.BlockSpec((None, 1, d), lambda i, f: (i // tiles_per_seq, 0, 0))
    vec = pl.BlockSpec((1, d), lambda i, f: (0, 0))
    return pl.pallas_call(
        _ffn_kernel,
        grid=(t // tm, n_f),
        in_specs=[
            pl.BlockSpec((tm, d), lambda i, f: (i, 0)),
            vec, batch_vec, batch_vec,
            pl.BlockSpec((d, tf), lambda i, f: (0, f)),
            pl.BlockSpec((d, tf), lambda i, f: (0, n_f + f)),
            pl.BlockSpec((tf, d), lambda i, f: (f, 0)),
            batch_vec, vec,
        ],
        out_specs=pl.BlockSpec((tm, d), lambda i, f: (i, 0)),
        out_shape=jax.ShapeDtypeStruct((t, d), F32),
        scratch_shapes=[pltpu.VMEM((tm, d), BF16), pltpu.VMEM((tm, d), F32)],
        compiler_params=_params(2),
        name="ffn",
    )(x2d, g_in, shift, scale, w_in, w_in, w_out, gate, g_out)


def _gelu_tanh(x):
    return 0.5 * x * (1.0 + jnp.tanh(np.sqrt(2.0 / np.pi) * (x + 0.044715 * (x * x * x))))


def _compress_kernel(c_ref, w1_ref, w2_ref, pe_ref, o_ref):
    half = CMP_STRIDE * HEAD_DIM
    c = c_ref[...]
    n_chunks = c.shape[0]
    first = _dot(c, w1_ref[0:half, :])
    second = _dot(c, w1_ref[half:2 * half, :])
    pe = jnp.broadcast_to(pe_ref[...], (8, 2 * half)).astype(BF16)
    bias = _dot(pe, w1_ref[...])[0:1, :]
    hidden = first + pltpu.roll(second, n_chunks - 1, axis=0) + bias
    out = _dot(_gelu_tanh(hidden).astype(BF16), w2_ref[...])
    row = lax.broadcasted_iota(jnp.int32, out.shape, 0)
    o_ref[...] = jnp.where(row < n_chunks - 1, out, 0.0).astype(o_ref.dtype)


def _compress(chunks, w1, w2, pe):
    _, b, g, n_chunks, width = chunks.shape
    return pl.pallas_call(
        _compress_kernel,
        grid=(2, b, g),
        in_specs=[
            pl.BlockSpec((None, None, None, n_chunks, width), lambda w, bi, gi: (w, bi, gi, 0, 0)),
            pl.BlockSpec((None, 2 * width, CMP_HIDDEN), lambda w, bi, gi: (w, 0, 0)),
            pl.BlockSpec((None, CMP_HIDDEN, HEAD_DIM), lambda w, bi, gi: (w, 0, 0)),
            pl.BlockSpec((None, 1, 2 * width), lambda w, bi, gi: (w, 0, 0)),
        ],
        out_specs=pl.BlockSpec((None, None, None, n_chunks, HEAD_DIM),
                               lambda w, bi, gi: (w, bi, gi, 0, 0)),
        out_shape=jax.ShapeDtypeStruct((2, b, g, n_chunks, HEAD_DIM), BF16),
        compiler_params=_params(3),
        name="compress",
    )(chunks, w1, w2, pe)


def _nsa_kernel(q_ref, kc_ref, vc_ref, ks_ref, vs_ref, kw_ref, vw_ref, gate_ref, o_ref,
                kaug_ref, qaug_ref, m_ref, l_ref, acc_ref, side_ref, sa_ref, sb_ref):
    tq = NSA_Q_TILE
    tk = NSA_K_TILE
    r_heads = HEADS_PER_GROUP
    rows = r_heads * tq
    seq = ks_ref.shape[0]
    n_cmp = kc_ref.shape[0]
    qi = pl.program_id(2)
    t0 = qi * tq

    @pl.when(qi == 0)
    def _():
        kaug_ref[:, 0:HEAD_DIM] = ks_ref[...]
        row = lax.broadcasted_iota(jnp.int32, (seq, LANES), 0)
        lane = lax.broadcasted_iota(jnp.int32, (seq, LANES), 1)
        kaug_ref[:, HEAD_DIM:2 * HEAD_DIM] = jnp.where(
            (row // SEL_BLOCK) == lane, 1.0, 0.0).astype(BF16)

    for r in range(r_heads):
        qaug_ref[r * tq:(r + 1) * tq, 0:HEAD_DIM] = q_ref[:, r * HEAD_DIM:(r + 1) * HEAD_DIM]
    q_all = qaug_ref[:, 0:HEAD_DIM]
    t_row = t0 + (lax.broadcasted_iota(jnp.int32, (rows, 1), 0) & (tq - 1))
    gates = gate_ref[...]

    def gate_col(branch):
        return jnp.concatenate([gates[:, 3 * r + branch:3 * r + branch + 1] for r in range(r_heads)], axis=0)

    span = WINDOW + tq
    w_start = pl.multiple_of(jnp.maximum(t0 - WINDOW, 0), tq)
    s_c = _dot_nt(q_all, kc_ref[...])
    s_w = _dot_nt(q_all, kw_ref[pl.ds(w_start, span), :])

    n_idx = lax.broadcasted_iota(jnp.int32, (rows, n_cmp), 1)
    valid = (n_idx * CMP_STRIDE + (CMP_BLOCK - 1)) <= t_row
    any_valid = jnp.where(t_row >= CMP_BLOCK - 1, 1.0, 0.0).astype(F32)
    s_c = jnp.where(valid, s_c, NEG)
    e_c = jnp.exp(s_c - jnp.max(s_c, axis=-1, keepdims=True))
    p_c = e_c / jnp.sum(e_c, axis=-1, keepdims=True) * any_valid
    o_cmp = _dot(p_c.astype(BF16), vc_ref[...])
    p_sum = p_c[0:tq]
    for r in range(1, r_heads):
        p_sum = p_sum + p_c[r * tq:(r + 1) * tq]

    kpos = w_start + lax.broadcasted_iota(jnp.int32, (rows, span), 1)
    in_window = (kpos <= t_row) & (kpos > t_row - WINDOW)
    s_w = jnp.where(in_window, s_w, NEG)
    e_w = jnp.exp(s_w - jnp.max(s_w, axis=-1, keepdims=True))
    o_win = _dot(e_w.astype(BF16), vw_ref[pl.ds(w_start, span), :]) / jnp.sum(e_w, axis=-1, keepdims=True)
    side_ref[...] = gate_col(0) * o_cmp + gate_col(2) * o_win

    n_blocks = seq // SEL_BLOCK
    n_sel = min(SEL_TOPK, n_blocks)
    m_o = lax.broadcasted_iota(jnp.int32, (LANES, n_cmp), 0)
    n_o = lax.broadcasted_iota(jnp.int32, (LANES, n_cmp), 1)
    overlap_t = jnp.where(n_o * CMP_STRIDE < (m_o + 1) * SEL_BLOCK,
                          jnp.where(n_o * CMP_STRIDE + CMP_BLOCK > m_o * SEL_BLOCK, 1.0, 0.0),
                          0.0).astype(BF16)
    p_hi, p_lo = _split_bf16(p_sum)
    imp = (_dot_nt(overlap_t, p_hi) + _dot_nt(overlap_t, p_lo))[0:n_blocks]
    blk = lax.broadcasted_iota(jnp.int32, (n_blocks, tq), 0)
    t_s = t0 + lax.broadcasted_iota(jnp.int32, (n_blocks, tq), 1)
    cur = t_s // SEL_BLOCK
    forced = (blk == 0) | (blk == cur) | (blk == cur - 1)
    avail = blk * SEL_BLOCK <= t_s
    imp = jnp.where(forced, FORCE, jnp.where(avail, imp, -FORCE))
    sub = 8
    imp_v = [imp[v * sub:(v + 1) * sub] for v in range(n_blocks // sub)]
    blk_v = lax.broadcasted_iota(jnp.int32, (sub, tq), 0)
    rank_v = [jnp.zeros((sub, tq), F32) for _ in imp_v]
    for mp in range(n_blocks):
        other = imp[mp:mp + 1]
        for v in range(len(imp_v)):
            wins_ties = jnp.where(other >= imp_v[v], 1.0, 0.0)
            loses_ties = jnp.where(other > imp_v[v], 1.0, 0.0)
            if v * sub > mp:
                ahead = wins_ties
            elif (v + 1) * sub - 1 <= mp:
                ahead = loses_ties
            else:
                ahead = jnp.where(blk_v + v * sub > mp, wins_ties, loses_ties)
            rank_v[v] = rank_v[v] + ahead
    rank = jnp.concatenate(rank_v, axis=0)
    mask_t = jnp.where(rank < n_sel, 0.0, UNSELECTED)
    mask_t = jnp.concatenate([mask_t, jnp.zeros((LANES - n_blocks, tq), F32)], axis=0)
    mask_feat = mask_t.T.astype(BF16)
    for r in range(r_heads):
        qaug_ref[r * tq:(r + 1) * tq, HEAD_DIM:2 * HEAD_DIM] = mask_feat

    m_ref[...] = jnp.full_like(m_ref, NEG)
    l_ref[...] = jnp.zeros_like(l_ref)
    acc_ref[...] = jnp.zeros_like(acc_ref)
    def scores(j, dst_ref):
        start = pl.multiple_of(j * tk, tk)
        dst_ref[...] = _dot_nt(qaug_ref[...], kaug_ref[pl.ds(start, tk), :])

    def consume(src_ref, j, causal):
        start = pl.multiple_of(j * tk, tk)
        s = src_ref[...]
        if causal:
            kp = start + lax.broadcasted_iota(jnp.int32, (rows, tk), 1)
            s = jnp.where(kp <= t_row, s, NEG)
        m_old = m_ref[...]
        m_new = jnp.maximum(m_old, jnp.max(s, axis=-1, keepdims=True))
        alpha = jnp.exp(m_old - m_new)
        p = jnp.exp(s - m_new[:, 0:1])
        l_ref[...] = alpha * l_ref[...] + jnp.sum(p, axis=-1, keepdims=True)
        acc_ref[...] = alpha * acc_ref[...] + _dot(p.astype(BF16), vs_ref[pl.ds(start, tk), :])
        m_ref[...] = m_new

    n_full = t0 // tk
    n_pairs = n_full // 2
    scores(0, sa_ref)

    def pair_body(i, carry):
        j = 2 * i
        scores(j + 1, sb_ref)
        consume(sa_ref, j, False)
        scores(j + 2, sa_ref)
        consume(sb_ref, j + 1, False)
        return carry

    lax.fori_loop(0, n_pairs, pair_body, 0)
    j_next = 2 * n_pairs

    @pl.when(j_next < n_full)
    def _():
        scores(j_next + 1, sb_ref)
        consume(sa_ref, j_next, False)
        consume(sb_ref, j_next + 1, True)

    @pl.when(j_next == n_full)
    def _():
        consume(sa_ref, j_next, True)

    out = side_ref[...] + gate_col(1) * (acc_ref[...] / l_ref[...])
    for r in range(r_heads):
        o_ref[:, r * HEAD_DIM:(r + 1) * HEAD_DIM] = out[r * tq:(r + 1) * tq].astype(o_ref.dtype)


def _nsa_attention(proj, comp, gates, batch, seq):
    t = proj.shape[0]
    tq = NSA_Q_TILE
    n_q = seq // tq
    rows = HEADS_PER_GROUP * tq
    n_chunks = comp.shape[3]
    kv_cols = NSA_KV_HEADS
    q_cols = N_HEADS

    def kv_spec(which):
        base = q_cols + which * kv_cols
        return pl.BlockSpec((seq, HEAD_DIM), lambda b, g, qi: (b, base + g))

    def cmp_spec(which):
        return pl.BlockSpec((None, None, None, n_chunks, HEAD_DIM),
                            lambda b, g, qi: (which, b, g, 0, 0))

    return pl.pallas_call(
        _nsa_kernel,
        grid=(batch, NSA_KV_HEADS, n_q),
        in_specs=[
            pl.BlockSpec((tq, HEADS_PER_GROUP * HEAD_DIM), lambda b, g, qi: (b * n_q + qi, g)),
            cmp_spec(0), cmp_spec(1),
            kv_spec(2), kv_spec(3), kv_spec(4), kv_spec(5),
            pl.BlockSpec((tq, LANES), lambda b, g, qi: (b * n_q + qi, g)),
        ],
        out_specs=pl.BlockSpec((tq, HEADS_PER_GROUP * HEAD_DIM), lambda b, g, qi: (b * n_q + qi, g)),
        out_shape=jax.ShapeDtypeStruct((t, N_HEADS * HEAD_DIM), BF16),
        scratch_shapes=[
            pltpu.VMEM((seq, 2 * HEAD_DIM), BF16),
            pltpu.VMEM((rows, 2 * HEAD_DIM), BF16),
            pltpu.VMEM((rows, LANES), F32),
            pltpu.VMEM((rows, LANES), F32),
            pltpu.VMEM((rows, HEAD_DIM), F32),
            pltpu.VMEM((rows, HEAD_DIM), F32),
            pltpu.VMEM((rows, NSA_K_TILE), F32),
            pltpu.VMEM((rows, NSA_K_TILE), F32),
        ],
        compiler_params=_params(3),
        name="nsa_attention",
    )(proj, comp, comp, proj, proj, proj, proj, gates)


def _sb_kernel(q_ref, k_ref, v_ref, o_ref, carry_ref, acc_ref):
    tk = SB_TILE
    qi = pl.program_id(2)
    row = lax.broadcasted_iota(jnp.int32, (tk, tk), 0)
    col = lax.broadcasted_iota(jnp.int32, (tk, tk), 1)
    later = jnp.where(row > col, 1.0, 0.0).astype(BF16)
    carry_ref[...] = jnp.zeros_like(carry_ref)
    acc_ref[...] = jnp.zeros_like(acc_ref)

    def step(j, diagonal):
        start = pl.multiple_of(j * tk, tk)
        heads = range(SB_HEADS)
        hs = [slice(h * HEAD_DIM, (h + 1) * HEAD_DIM) for h in heads]
        causal = col < row
        z = [_dot_nt(q_ref[:, hs[h]], k_ref[pl.ds(start, tk), hs[h]]) for h in heads]
        sp = [jnp.maximum(z[h], 0.0) + jnp.log(1.0 + jnp.exp(-jnp.abs(z[h]))) for h in heads]
        log_beta = [z[h] - sp[h] for h in heads]
        if diagonal:
            sp = [jnp.where(causal, sp[h], 0.0) for h in heads]
        split = [_split_bf16(sp[h]) for h in heads]
        after = [_dot(split[h][0], later) + _dot(split[h][1], later) for h in heads]
        carry = [carry_ref[h] for h in heads]
        a = [jnp.exp(log_beta[h] - after[h] - jnp.concatenate([carry[h]] * (tk // LANES), axis=1))
             for h in heads]
        if diagonal:
            a = [jnp.where(causal, a[h], 0.0) for h in heads]
        for h in heads:
            acc_ref[h] += _dot(a[h].astype(BF16), v_ref[pl.ds(start, tk), hs[h]])
            carry_ref[h] = carry[h] + (after[h][:, 0:1] + sp[h][:, 0:1])

    step(qi, True)

    def body(i, c):
        step(qi - 1 - i, False)
        return c

    lax.fori_loop(0, qi, body, 0)
    for h in range(SB_HEADS):
        o_ref[:, h * HEAD_DIM:(h + 1) * HEAD_DIM] = acc_ref[h].astype(o_ref.dtype)


def _sb_attention(q, kv, batch, seq):
    t = q.shape[0]
    tq = SB_TILE
    n_q = seq // tq
    width = SB_HEADS * HEAD_DIM
    n_hb = N_HEADS // SB_HEADS
    return pl.pallas_call(
        _sb_kernel,
        grid=(batch, n_hb, n_q),
        in_specs=[
            pl.BlockSpec((tq, width), lambda b, h, qi: (b * n_q + qi, h)),
            pl.BlockSpec((seq, width), lambda b, h, qi: (b, h)),
            pl.BlockSpec((seq, width), lambda b, h, qi: (b, n_hb + h)),
        ],
        out_specs=pl.BlockSpec((tq, width), lambda b, h, qi: (b * n_q + qi, h)),
        out_shape=jax.ShapeDtypeStruct((t, N_HEADS * HEAD_DIM), BF16),
        scratch_shapes=[pltpu.VMEM((SB_HEADS, tq, LANES), F32),
                        pltpu.VMEM((SB_HEADS, tq, HEAD_DIM), F32)],
        compiler_params=_params(3),
        name="sb_attention",
    )(q, kv, kv)


def _rope_tables(seq):
    half = HEAD_DIM // 2
    inv = ROPE_THETA ** (-jnp.arange(half, dtype=F32) / half)
    ang = jnp.arange(seq, dtype=F32)[:, None] * inv[None, :]
    cos, sin = jnp.cos(ang), jnp.sin(ang)
    return jnp.concatenate([cos, cos], axis=-1), jnp.concatenate([-sin, sin], axis=-1)


def kernel(x, c, mod_w, mod_b, norm_g, ffn_w_in, ffn_w_out, a_w_in, a_gate_b, a_cmp_pe, a_cmp_w1,
           a_cmp_w2, a_w_out, b_w_q, b_w_out, kv_norm_g, kv_mod_w, kv_mod_b, kv_w):
    batch, seq, d = x.shape
    depth = mod_w.shape[0]
    n_a = a_w_in.shape[0]
    t = batch * seq
    hd = N_HEADS * HEAD_DIM
    kvd = NSA_KV_HEADS * HEAD_DIM
    x2d = x.reshape(t, d)

    c_pad = jnp.pad(c, ((0, 8 - batch), (0, 0)))
    mod = _modulation(c_pad, mod_w, mod_b[:, None, :])[:, :batch]
    mod = mod.reshape(depth, batch, 6, 1, d)
    kv_mod = _modulation(c_pad, kv_mod_w[None], kv_mod_b[None, None, :])[0, :batch]
    kv_mod = kv_mod.reshape(batch, 2, 1, d)
    cos, sin = _rope_tables(seq)

    kv = None
    for layer in range(depth):
        sh1, sc1, g1, sh2, sc2, g2 = (mod[layer, :, i] for i in range(6))
        gam = norm_g[layer][:, None, :]
        if layer < n_a:
            w_in = a_w_in[layer]
            n_main = hd + 6 * kvd
            proj = _project(x2d, seq, gam[0], sh1, sc1, w_in[:, :n_main].astype(BF16),
                            mode="nsa", extra=(cos, sin))
            n_gate = 3 * HEADS_PER_GROUP
            w_gate = w_in[:, n_main:].reshape(d, NSA_KV_HEADS, n_gate)
            w_gate = jnp.pad(w_gate, ((0, 0), (0, 0), (0, LANES - n_gate))).reshape(d, NSA_KV_HEADS * LANES)
            b_gate = jnp.pad(a_gate_b[layer].reshape(NSA_KV_HEADS, n_gate),
                             ((0, 0), (0, LANES - n_gate))).reshape(1, NSA_KV_HEADS * LANES)
            gates = _project(x2d, seq, gam[0], sh1, sc1, w_gate.astype(BF16), mode="gate",
                             out_dtype=F32, extra=(b_gate,), tn=NSA_KV_HEADS * LANES)
            n_chunks = seq // CMP_STRIDE
            chunks = proj[:, hd:hd + 2 * kvd].reshape(
                batch, n_chunks, CMP_STRIDE, 2, NSA_KV_HEADS, HEAD_DIM)
            chunks = chunks.transpose(3, 0, 4, 1, 2, 5).reshape(
                2, batch, NSA_KV_HEADS, n_chunks, CMP_STRIDE * HEAD_DIM)
            comp = _compress(chunks, a_cmp_w1[layer].astype(BF16), a_cmp_w2[layer].astype(BF16),
                             a_cmp_pe[layer].reshape(2, 1, CMP_BLOCK * HEAD_DIM))
            o = _nsa_attention(proj, comp, gates, batch, seq)
            x2d = _out_project(o, a_w_out[layer].astype(BF16), x2d, seq, g1, gam[1])
        else:
            jb = layer - n_a
            if kv is None:
                kv = _project(x2d, seq, kv_norm_g[None, :], kv_mod[:, 0], kv_mod[:, 1],
                              kv_w.astype(BF16))
            q = _project(x2d, seq, gam[0], sh1, sc1, b_w_q[jb].astype(BF16), out_scale=ATTN_SCALE)
            o = _sb_attention(q, kv, batch, seq)
            x2d = _out_project(o, b_w_out[jb].astype(BF16), x2d, seq, g1, gam[1])
        x2d = _ffn(x2d, seq, gam[2], sh2, sc2, ffn_w_in[layer].astype(BF16),
                   ffn_w_out[layer].astype(BF16), g2, gam[3])
    return x2d.reshape(batch, seq, d)
```

```python
import functools

import jax
import jax.numpy as jnp
import numpy as np
from jax import lax
from jax.experimental import pallas as pl
from jax.experimental.pallas import tpu as pltpu

N_HEADS = 16
HEAD_DIM = 128
NSA_KV_HEADS = 4
HEADS_PER_GROUP = N_HEADS // NSA_KV_HEADS
CMP_BLOCK = 32
CMP_STRIDE = 16
CMP_HIDDEN = 256
SEL_BLOCK = 64
SEL_TOPK = 16
WINDOW = 512
D_FF = 5632
ROPE_THETA = 10000.0
EPS = 1e-6
NEG = -1e30
FORCE = 1e9
ATTN_SCALE = HEAD_DIM ** -0.5
LOG2_E = float(np.log2(np.e))
Q_SCALE = ATTN_SCALE * LOG2_E
UNSELECTED = -(2.0 ** 100)

LANES = 128
BF16 = jnp.bfloat16
F32 = jnp.float32

ROW_TILE = 1024
OUT_ROW_TILE = 512
PROJ_COL_TILE = 512
FF_TILE = 512
NSA_Q_TILE = 128
NSA_K_TILE = 512
SB_TILE = 256
SB_HEADS = 4
MOD_COL_TILE = 1024
VMEM_LIMIT = 56 * 1024 * 1024

_NT = (((1,), (1,)), ((), ()))


def _params(n_axes):
    return pltpu.CompilerParams(
        dimension_semantics=("arbitrary",) * n_axes, vmem_limit_bytes=VMEM_LIMIT)


def _dot(a, b):
    return jnp.dot(a, b, preferred_element_type=F32)


def _dot_nt(a, b):
    return lax.dot_general(a, b, _NT, preferred_element_type=F32)


def _split_bf16(x):
    hi = x.astype(BF16)
    lo = (x - hi.astype(F32)).astype(BF16)
    return hi, lo


def _rms(x, gamma):
    ms = jnp.mean(x * x, axis=-1, keepdims=True)
    return x * lax.rsqrt(ms + EPS) * gamma


def _mod_kernel(c_ref, w_ref, b_ref, o_ref):
    c = c_ref[...]
    ca = c * jax.nn.sigmoid(c)
    c_hi, c_lo = _split_bf16(ca)
    w_hi, w_lo = _split_bf16(w_ref[...])
    acc = _dot(c_hi, w_hi) + _dot(c_lo, w_hi) + _dot(c_hi, w_lo)
    o_ref[...] = acc + b_ref[...]


def _modulation(c_pad, w, b):
    n_layers, d, n = w.shape
    tn = MOD_COL_TILE
    return pl.pallas_call(
        _mod_kernel,
        grid=(n_layers, n // tn),
        in_specs=[
            pl.BlockSpec((8, d), lambda l, j: (0, 0)),
            pl.BlockSpec((None, d, tn), lambda l, j: (l, 0, j)),
            pl.BlockSpec((None, 1, tn), lambda l, j: (l, 0, j)),
        ],
        out_specs=pl.BlockSpec((None, 8, tn), lambda l, j: (l, 0, j)),
        out_shape=jax.ShapeDtypeStruct((n_layers, 8, n), F32),
        compiler_params=_params(2),
        name="modulation",
    )(c_pad, w, b)


def _normmod(x_ref, g_ref, sh_ref, sc_ref):
    y = _rms(x_ref[...], g_ref[...])
    return y * (1.0 + sc_ref[...]) + sh_ref[...]


def _proj_kernel(x_ref, g_ref, sh_ref, sc_ref, w_ref, *rest, mode, out_scale):
    if mode == "nsa":
        cos_ref, sin_ref, o_ref, h_ref = rest
    elif mode == "gate":
        b_ref, o_ref, h_ref = rest
    else:
        o_ref, h_ref = rest
    j = pl.program_id(1)

    @pl.when(j == 0)
    def _():
        h_ref[...] = _normmod(x_ref, g_ref, sh_ref, sc_ref).astype(BF16)

    acc = _dot(h_ref[...], w_ref[...])
    if mode == "plain":
        o_ref[...] = (acc * out_scale).astype(o_ref.dtype)
    elif mode == "gate":
        o_ref[...] = jax.nn.sigmoid(acc + b_ref[...])
    else:
        is_q = j < 4
        is_rope = is_q | (j == 4) | (j == 6) | (j == 8)
        scale = jnp.where(is_q, Q_SCALE, 1.0).astype(F32)
        cos = jnp.where(is_rope, cos_ref[...] * scale, 1.0)
        sin = jnp.where(is_rope, sin_ref[...] * scale, 0.0)
        for hh in range(PROJ_COL_TILE // HEAD_DIM):
            xh = acc[:, hh * HEAD_DIM:(hh + 1) * HEAD_DIM]
            rot = xh * cos + pltpu.roll(xh, HEAD_DIM // 2, axis=1) * sin
            o_ref[:, hh * HEAD_DIM:(hh + 1) * HEAD_DIM] = rot.astype(o_ref.dtype)


def _project(x2d, seq, gamma, shift, scale, w, *, mode="plain", out_scale=1.0,
             out_dtype=BF16, extra=(), tn=PROJ_COL_TILE):
    t, d = x2d.shape
    n = w.shape[1]
    tm = min(ROW_TILE, seq)
    tiles_per_seq = seq // tm
    in_specs = [
        pl.BlockSpec((tm, d), lambda i, j: (i, 0)),
        pl.BlockSpec((1, d), lambda i, j: (0, 0)),
        pl.BlockSpec((None, 1, d), lambda i, j: (i // tiles_per_seq, 0, 0)),
        pl.BlockSpec((None, 1, d), lambda i, j: (i // tiles_per_seq, 0, 0)),
        pl.BlockSpec((d, tn), lambda i, j: (0, j)),
    ]
    if mode == "nsa":
        in_specs += [pl.BlockSpec((tm, HEAD_DIM), lambda i, j: (i % tiles_per_seq, 0))] * 2
    elif mode == "gate":
        in_specs += [pl.BlockSpec((1, tn), lambda i, j: (0, j))]
    return pl.pallas_call(
        functools.partial(_proj_kernel, mode=mode, out_scale=out_scale),
        grid=(t // tm, n // tn),
        in_specs=in_specs,
        out_specs=pl.BlockSpec((tm, tn), lambda i, j: (i, j)),
        out_shape=jax.ShapeDtypeStruct((t, n), out_dtype),
        scratch_shapes=[pltpu.VMEM((tm, d), BF16)],
        compiler_params=_params(2),
        name="proj_" + mode,
    )(x2d, gamma, shift, scale, w, *extra)


def _outproj_kernel(o_ref, w_ref, x_ref, gate_ref, g_ref, out_ref):
    y = _dot(o_ref[...], w_ref[...])
    out_ref[...] = x_ref[...] + gate_ref[...] * _rms(y, g_ref[...])


def _out_project(o2d, w, x2d, seq, gate, gamma):
    t, d = x2d.shape
    k = o2d.shape[1]
    tm = min(OUT_ROW_TILE, seq)
    tiles_per_seq = seq // tm
    return pl.pallas_call(
        _outproj_kernel,
        grid=(t // tm,),
        in_specs=[
            pl.BlockSpec((tm, k), lambda i: (i, 0)),
            pl.BlockSpec((k, d), lambda i: (0, 0)),
            pl.BlockSpec((tm, d), lambda i: (i, 0)),
            pl.BlockSpec((None, 1, d), lambda i: (i // tiles_per_seq, 0, 0)),
            pl.BlockSpec((1, d), lambda i: (0, 0)),
        ],
        out_specs=pl.BlockSpec((tm, d), lambda i: (i, 0)),
        out_shape=jax.ShapeDtypeStruct((t, d), F32),
        compiler_params=_params(1),
        name="out_proj",
    )(o2d, w, x2d, gate, gamma)


def _ffn_kernel(x_ref, g_in_ref, sh_ref, sc_ref, wg_ref, wu_ref, wo_ref, gate_ref, g_out_ref,
                out_ref, h_ref, acc_ref):
    f = pl.program_id(1)

    @pl.when(f == 0)
    def _():
        h_ref[...] = _normmod(x_ref, g_in_ref, sh_ref, sc_ref).astype(BF16)
        acc_ref[...] = jnp.zeros_like(acc_ref)

    h = h_ref[...]
    gate = _dot(h, wg_ref[...])
    up = _dot(h, wu_ref[...])
    act = (gate * jax.nn.sigmoid(gate) * up).astype(BF16)
    acc_ref[...] += _dot(act, wo_ref[...])

    @pl.when(f == pl.num_programs(1) - 1)
    def _():
        out_ref[...] = x_ref[...] + gate_ref[...] * _rms(acc_ref[...], g_out_ref[...])


def _ffn(x2d, seq, g_in, shift, scale, w_in, w_out, gate, g_out):
    t, d = x2d.shape
    d_ff = w_out.shape[0]
    tm = min(OUT_ROW_TILE, seq)
    tf = FF_TILE
    n_f = d_ff // tf
    tiles_per_seq = seq // tm
    batch_vec = pl.BlockSpec((None, 1, d), lambda i, f: (i // tiles_per_seq, 0, 0))
    vec = pl.BlockSpec((1, d), lambda i, f: (0, 0))
    return pl.pallas_call(
        _ffn_kernel,
        grid=(t // tm, n_f),
        in_specs=[
            pl.BlockSpec((tm, d), lambda i, f: (i, 0)),
            vec, batch_vec, batch_vec,
            pl.BlockSpec((d, tf), lambda i, f: (0, f)),
            pl.BlockSpec((d, tf), lambda i, f: (0, n_f + f)),
            pl.BlockSpec((tf, d), lambda i, f: (f, 0)),
            batch_vec, vec,
        ],
        out_specs=pl.BlockSpec((tm, d), lambda i, f: (i, 0)),
        out_shape=jax.ShapeDtypeStruct((t, d), F32),
        scratch_shapes=[pltpu.VMEM((tm, d), BF16), pltpu.VMEM((tm, d), F32)],
        compiler_params=_params(2),
        name="ffn",
    )(x2d, g_in, shift, scale, w_in, w_in, w_out, gate, g_out)


def _gelu_tanh(x):
    return 0.5 * x * (1.0 + jnp.tanh(np.sqrt(2.0 / np.pi) * (x + 0.044715 * (x * x * x))))


def _compress_kernel(c_ref, w1_ref, w2_ref, pe_ref, o_ref):
    half = CMP_STRIDE * HEAD_DIM
    c = c_ref[...]
    n_chunks = c.shape[0]
    first = _dot(c, w1_ref[0:half, :])
    second = _dot(c, w1_ref[half:2 * half, :])
    pe = jnp.broadcast_to(pe_ref[...], (8, 2 * half)).astype(BF16)
    bias = _dot(pe, w1_ref[...])[0:1, :]
    hidden = first + pltpu.roll(second, n_chunks - 1, axis=0) + bias
    out = _dot(_gelu_tanh(hidden).astype(BF16), w2_ref[...])
    row = lax.broadcasted_iota(jnp.int32, out.shape, 0)
    o_ref[...] = jnp.where(row < n_chunks - 1, out, 0.0).astype(o_ref.dtype)


def _compress(chunks, w1, w2, pe):
    _, b, g, n_chunks, width = chunks.shape
    return pl.pallas_call(
        _compress_kernel,
        grid=(2, b, g),
        in_specs=[
            pl.BlockSpec((None, None, None, n_chunks, width), lambda w, bi, gi: (w, bi, gi, 0, 0)),
            pl.BlockSpec((None, 2 * width, CMP_HIDDEN), lambda w, bi, gi: (w, 0, 0)),
            pl.BlockSpec((None, CMP_HIDDEN, HEAD_DIM), lambda w, bi, gi: (w, 0, 0)),
            pl.BlockSpec((None, 1, 2 * width), lambda w, bi, gi: (w, 0, 0)),
        ],
        out_specs=pl.BlockSpec((None, None, None, n_chunks, HEAD_DIM),
                               lambda w, bi, gi: (w, bi, gi, 0, 0)),
        out_shape=jax.ShapeDtypeStruct((2, b, g, n_chunks, HEAD_DIM), BF16),
        compiler_params=_params(3),
        name="compress",
    )(chunks, w1, w2, pe)


def _nsa_kernel(q_ref, kc_ref, vc_ref, ks_ref, vs_ref, kw_ref, vw_ref, gate_ref, o_ref,
                kaug_ref, qaug_ref, m_ref, l_ref, acc_ref, side_ref, sa_ref, sb_ref):
    tq = NSA_Q_TILE
    tk = NSA_K_TILE
    r_heads = HEADS_PER_GROUP
    rows = r_heads * tq
    seq = ks_ref.shape[0]
    n_cmp = kc_ref.shape[0]
    qi = pl.program_id(2)
    t0 = qi * tq

    @pl.when(qi == 0)
    def _():
        kaug_ref[:, 0:HEAD_DIM] = ks_ref[...]
        row = lax.broadcasted_iota(jnp.int32, (seq, LANES), 0)
        lane = lax.broadcasted_iota(jnp.int32, (seq, LANES), 1)
        kaug_ref[:, HEAD_DIM:2 * HEAD_DIM] = jnp.where(
            (row // SEL_BLOCK) == lane, 1.0, 0.0).astype(BF16)

    for r in range(r_heads):
        qaug_ref[r * tq:(r + 1) * tq, 0:HEAD_DIM] = q_ref[:, r * HEAD_DIM:(r + 1) * HEAD_DIM]
    q_all = qaug_ref[:, 0:HEAD_DIM]
    t_row = t0 + (lax.broadcasted_iota(jnp.int32, (rows, 1), 0) & (tq - 1))
    gates = gate_ref[...]

    def gate_col(branch):
        return jnp.concatenate([gates[:, 3 * r + branch:3 * r + branch + 1] for r in range(r_heads)], axis=0)

    span = WINDOW + tq
    w_start = pl.multiple_of(jnp.maximum(t0 - WINDOW, 0), tq)
    s_c = _dot_nt(q_all, kc_ref[...])
    s_w = _dot_nt(q_all, kw_ref[pl.ds(w_start, span), :])

    n_idx = lax.broadcasted_iota(jnp.int32, (rows, n_cmp), 1)
    valid = (n_idx * CMP_STRIDE + (CMP_BLOCK - 1)) <= t_row
    any_valid = jnp.where(t_row >= CMP_BLOCK - 1, 1.0, 0.0).astype(F32)
    s_c = jnp.where(valid, s_c, NEG)
    e_c = jnp.exp2(s_c - jnp.max(s_c, axis=-1, keepdims=True))
    p_c = e_c / jnp.sum(e_c, axis=-1, keepdims=True) * any_valid
    o_cmp = _dot(p_c.astype(BF16), vc_ref[...])
    p_sum = p_c[0:tq]
    for r in range(1, r_heads):
        p_sum = p_sum + p_c[r * tq:(r + 1) * tq]

    kpos = w_start + lax.broadcasted_iota(jnp.int32, (rows, span), 1)
    in_window = (kpos <= t_row) & (kpos > t_row - WINDOW)
    s_w = jnp.where(in_window, s_w, NEG)
    e_w = jnp.exp2(s_w - jnp.max(s_w, axis=-1, keepdims=True))
    o_win = _dot(e_w.astype(BF16), vw_ref[pl.ds(w_start, span), :]) / jnp.sum(e_w, axis=-1, keepdims=True)
    side_ref[...] = gate_col(0) * o_cmp + gate_col(2) * o_win

    n_blocks = seq // SEL_BLOCK
    n_sel = min(SEL_TOPK, n_blocks)
    m_o = lax.broadcasted_iota(jnp.int32, (LANES, n_cmp), 0)
    n_o = lax.broadcasted_iota(jnp.int32, (LANES, n_cmp), 1)
    overlap_t = jnp.where(n_o * CMP_STRIDE < (m_o + 1) * SEL_BLOCK,
                          jnp.where(n_o * CMP_STRIDE + CMP_BLOCK > m_o * SEL_BLOCK, 1.0, 0.0),
                          0.0).astype(BF16)
    p_hi, p_lo = _split_bf16(p_sum)
    imp = (_dot_nt(overlap_t, p_hi) + _dot_nt(overlap_t, p_lo))[0:n_blocks]
    blk = lax.broadcasted_iota(jnp.int32, (n_blocks, tq), 0)
    t_s = t0 + lax.broadcasted_iota(jnp.int32, (n_blocks, tq), 1)
    cur = t_s // SEL_BLOCK
    forced = (blk == 0) | (blk == cur) | (blk == cur - 1)
    avail = blk * SEL_BLOCK <= t_s
    imp = jnp.where(forced, FORCE, jnp.where(avail, imp, -FORCE))
    sub = 8
    imp_v = [imp[v * sub:(v + 1) * sub] for v in range(n_blocks // sub)]
    blk_v = lax.broadcasted_iota(jnp.int32, (sub, tq), 0)
    rank_v = [jnp.zeros((sub, tq), F32) for _ in imp_v]
    for mp in range(n_blocks):
        other = imp[mp:mp + 1]
        for v in range(len(imp_v)):
            wins_ties = jnp.where(other >= imp_v[v], 1.0, 0.0)
            loses_ties = jnp.where(other > imp_v[v], 1.0, 0.0)
            if v * sub > mp:
                ahead = wins_ties
            elif (v + 1) * sub - 1 <= mp:
                ahead = loses_ties
            else:
                ahead = jnp.where(blk_v + v * sub > mp, wins_ties, loses_ties)
            rank_v[v] = rank_v[v] + ahead
    rank = jnp.concatenate(rank_v, axis=0)
    mask_t = jnp.where(rank < n_sel, 0.0, UNSELECTED)
    mask_t = jnp.concatenate([mask_t, jnp.zeros((LANES - n_blocks, tq), F32)], axis=0)
    mask_feat = mask_t.T.astype(BF16)
    for r in range(r_heads):
        qaug_ref[r * tq:(r + 1) * tq, HEAD_DIM:2 * HEAD_DIM] = mask_feat

    m_ref[...] = jnp.full_like(m_ref, NEG)
    l_ref[...] = jnp.zeros_like(l_ref)
    acc_ref[...] = jnp.zeros_like(acc_ref)
    def scores(j, dst_ref):
        start = pl.multiple_of(j * tk, tk)
        dst_ref[...] = _dot_nt(qaug_ref[...], kaug_ref[pl.ds(start, tk), :])

    def consume(src_ref, j, causal):
        start = pl.multiple_of(j * tk, tk)
        s = src_ref[...]
        if causal:
            kp = start + lax.broadcasted_iota(jnp.int32, (rows, tk), 1)
            s = jnp.where(kp <= t_row, s, NEG)
        m_old = m_ref[...]
        m_new = jnp.maximum(m_old, jnp.max(s, axis=-1, keepdims=True))
        alpha = jnp.exp2(m_old - m_new)
        p = jnp.exp2(s - m_new[:, 0:1])
        l_ref[...] = alpha * l_ref[...] + jnp.sum(p, axis=-1, keepdims=True)
        acc_ref[...] = alpha * acc_ref[...] + _dot(p.astype(BF16), vs_ref[pl.ds(start, tk), :])
        m_ref[...] = m_new

    n_full = t0 // tk
    n_pairs = n_full // 2
    scores(0, sa_ref)

    def pair_body(i, carry):
        j = 2 * i
        scores(j + 1, sb_ref)
        consume(sa_ref, j, False)
        scores(j + 2, sa_ref)
        consume(sb_ref, j + 1, False)
        return carry

    lax.fori_loop(0, n_pairs, pair_body, 0)
    j_next = 2 * n_pairs

    @pl.when(j_next < n_full)
    def _():
        scores(j_next + 1, sb_ref)
        consume(sa_ref, j_next, False)
        consume(sb_ref, j_next + 1, True)

    @pl.when(j_next == n_full)
    def _():
        consume(sa_ref, j_next, True)

    out = side_ref[...] + gate_col(1) * (acc_ref[...] / l_ref[...])
    for r in range(r_heads):
        o_ref[:, r * HEAD_DIM:(r + 1) * HEAD_DIM] = out[r * tq:(r + 1) * tq].astype(o_ref.dtype)


def _nsa_attention(proj, comp, gates, batch, seq):
    t = proj.shape[0]
    tq = NSA_Q_TILE
    n_q = seq // tq
    rows = HEADS_PER_GROUP * tq
    n_chunks = comp.shape[3]
    kv_cols = NSA_KV_HEADS
    q_cols = N_HEADS

    def kv_spec(which):
        base = q_cols + which * kv_cols
        return pl.BlockSpec((seq, HEAD_DIM), lambda b, g, qi: (b, base + g))

    def cmp_spec(which):
        return pl.BlockSpec((None, None, None, n_chunks, HEAD_DIM),
                            lambda b, g, qi: (which, b, g, 0, 0))

    return pl.pallas_call(
        _nsa_kernel,
        grid=(batch, NSA_KV_HEADS, n_q),
        in_specs=[
            pl.BlockSpec((tq, HEADS_PER_GROUP * HEAD_DIM), lambda b, g, qi: (b * n_q + qi, g)),
            cmp_spec(0), cmp_spec(1),
            kv_spec(2), kv_spec(3), kv_spec(4), kv_spec(5),
            pl.BlockSpec((tq, LANES), lambda b, g, qi: (b * n_q + qi, g)),
        ],
        out_specs=pl.BlockSpec((tq, HEADS_PER_GROUP * HEAD_DIM), lambda b, g, qi: (b * n_q + qi, g)),
        out_shape=jax.ShapeDtypeStruct((t, N_HEADS * HEAD_DIM), BF16),
        scratch_shapes=[
            pltpu.VMEM((seq, 2 * HEAD_DIM), BF16),
            pltpu.VMEM((rows, 2 * HEAD_DIM), BF16),
            pltpu.VMEM((rows, LANES), F32),
            pltpu.VMEM((rows, LANES), F32),
            pltpu.VMEM((rows, HEAD_DIM), F32),
            pltpu.VMEM((rows, HEAD_DIM), F32),
            pltpu.VMEM((rows, NSA_K_TILE), F32),
            pltpu.VMEM((rows, NSA_K_TILE), F32),
        ],
        compiler_params=_params(3),
        name="nsa_attention",
    )(proj, comp, comp, proj, proj, proj, proj, gates)


def _sb_kernel(q_ref, k_ref, v_ref, o_ref, carry_ref, acc_ref, za_ref, zb_ref):
    tk = SB_TILE
    qi = pl.program_id(2)
    row = lax.broadcasted_iota(jnp.int32, (tk, tk), 0)
    col = lax.broadcasted_iota(jnp.int32, (tk, tk), 1)
    causal = col < row
    not_before = jnp.where(row >= col, 1.0, 0.0).astype(BF16)
    not_before2 = jnp.concatenate([not_before, not_before], axis=0)
    carry_ref[...] = jnp.zeros_like(carry_ref)
    acc_ref[...] = jnp.zeros_like(acc_ref)
    sign_bit = jnp.uint32(0x80000000)

    heads = range(SB_HEADS)
    hs = [slice(h * HEAD_DIM, (h + 1) * HEAD_DIM) for h in heads]

    def scores(j, dst_ref):
        start = pl.multiple_of(j * tk, tk)
        for h in heads:
            dst_ref[h] = _dot_nt(q_ref[:, hs[h]], k_ref[pl.ds(start, tk), hs[h]])

    def consume(src_ref, j, diagonal):
        start = pl.multiple_of(j * tk, tk)
        z = [src_ref[h] for h in heads]
        neg_abs = [pltpu.bitcast(pltpu.bitcast(z[h], jnp.uint32) | sign_bit, F32) for h in heads]
        sp = [jnp.maximum(z[h], 0.0) + jnp.log(1.0 + jnp.exp2(neg_abs[h])) * LOG2_E for h in heads]
        if diagonal:
            sp = [jnp.where(causal, sp[h], 0.0) for h in heads]
        split = [jnp.concatenate(_split_bf16(sp[h]), axis=1) for h in heads]
        suffix = [_dot(split[h], not_before2) for h in heads]
        carry = [carry_ref[h] for h in heads]
        a = [jnp.exp2(z[h] - suffix[h] - jnp.concatenate([carry[h]] * (tk // LANES), axis=1))
             for h in heads]
        if diagonal:
            a = [jnp.where(causal, a[h], 0.0) for h in heads]
        for h in heads:
            acc_ref[h] += _dot(a[h].astype(BF16), v_ref[pl.ds(start, tk), hs[h]])
            carry_ref[h] = carry[h] + suffix[h][:, 0:1]

    scores(qi, za_ref)
    scores(jnp.maximum(qi - 1, 0), zb_ref)
    consume(za_ref, qi, True)
    n_pairs = jnp.maximum(qi - 1, 0) // 2

    def pair_body(i, c):
        j = qi - 1 - 2 * i
        scores(j - 1, za_ref)
        consume(zb_ref, j, False)
        scores(j - 2, zb_ref)
        consume(za_ref, j - 1, False)
        return c

    lax.fori_loop(0, n_pairs, pair_body, 0)
    j_next = qi - 1 - 2 * n_pairs

    @pl.when(j_next == 1)
    def _():
        scores(0, za_ref)
        consume(zb_ref, 1, False)
        consume(za_ref, 0, False)

    @pl.when(j_next == 0)
    def _():
        consume(zb_ref, 0, False)

    for h in range(SB_HEADS):
        o_ref[:, h * HEAD_DIM:(h + 1) * HEAD_DIM] = acc_ref[h].astype(o_ref.dtype)


def _sb_attention(q, kv, batch, seq):
    t = q.shape[0]
    tq = SB_TILE
    n_q = seq // tq
    width = SB_HEADS * HEAD_DIM
    n_hb = N_HEADS // SB_HEADS
    return pl.pallas_call(
        _sb_kernel,
        grid=(batch, n_hb, n_q),
        in_specs=[
            pl.BlockSpec((tq, width), lambda b, h, qi: (b * n_q + qi, h)),
            pl.BlockSpec((seq, width), lambda b, h, qi: (b, h)),
            pl.BlockSpec((seq, width), lambda b, h, qi: (b, n_hb + h)),
        ],
        out_specs=pl.BlockSpec((tq, width), lambda b, h, qi: (b * n_q + qi, h)),
        out_shape=jax.ShapeDtypeStruct((t, N_HEADS * HEAD_DIM), BF16),
        scratch_shapes=[pltpu.VMEM((SB_HEADS, tq, LANES), F32),
                        pltpu.VMEM((SB_HEADS, tq, HEAD_DIM), F32),
                        pltpu.VMEM((SB_HEADS, tq, SB_TILE), F32),
                        pltpu.VMEM((SB_HEADS, tq, SB_TILE), F32)],
        compiler_params=_params(3),
        name="sb_attention",
    )(q, kv, kv)


def _rope_tables(seq):
    half = HEAD_DIM // 2
    inv = ROPE_THETA ** (-jnp.arange(half, dtype=F32) / half)
    ang = jnp.arange(seq, dtype=F32)[:, None] * inv[None, :]
    cos, sin = jnp.cos(ang), jnp.sin(ang)
    return jnp.concatenate([cos, cos], axis=-1), jnp.concatenate([-sin, sin], axis=-1)


def kernel(x, c, mod_w, mod_b, norm_g, ffn_w_in, ffn_w_out, a_w_in, a_gate_b, a_cmp_pe, a_cmp_w1,
           a_cmp_w2, a_w_out, b_w_q, b_w_out, kv_norm_g, kv_mod_w, kv_mod_b, kv_w):
    batch, seq, d = x.shape
    depth = mod_w.shape[0]
    n_a = a_w_in.shape[0]
    t = batch * seq
    hd = N_HEADS * HEAD_DIM
    kvd = NSA_KV_HEADS * HEAD_DIM
    x2d = x.reshape(t, d)

    c_pad = jnp.pad(c, ((0, 8 - batch), (0, 0)))
    mod = _modulation(c_pad, mod_w, mod_b[:, None, :])[:, :batch]
    mod = mod.reshape(depth, batch, 6, 1, d)
    kv_mod = _modulation(c_pad, kv_mod_w[None], kv_mod_b[None, None, :])[0, :batch]
    kv_mod = kv_mod.reshape(batch, 2, 1, d)
    cos, sin = _rope_tables(seq)

    kv = None
    for layer in range(depth):
        sh1, sc1, g1, sh2, sc2, g2 = (mod[layer, :, i] for i in range(6))
        gam = norm_g[layer][:, None, :]
        if layer < n_a:
            w_in = a_w_in[layer]
            n_main = hd + 6 * kvd
            proj = _project(x2d, seq, gam[0], sh1, sc1, w_in[:, :n_main].astype(BF16),
                            mode="nsa", extra=(cos, sin))
            n_gate = 3 * HEADS_PER_GROUP
            w_gate = w_in[:, n_main:].reshape(d, NSA_KV_HEADS, n_gate)
            w_gate = jnp.pad(w_gate, ((0, 0), (0, 0), (0, LANES - n_gate))).reshape(d, NSA_KV_HEADS * LANES)
            b_gate = jnp.pad(a_gate_b[layer].reshape(NSA_KV_HEADS, n_gate),
                             ((0, 0), (0, LANES - n_gate))).reshape(1, NSA_KV_HEADS * LANES)
            gates = _project(x2d, seq, gam[0], sh1, sc1, w_gate.astype(BF16), mode="gate",
                             out_dtype=F32, extra=(b_gate,), tn=NSA_KV_HEADS * LANES)
            n_chunks = seq // CMP_STRIDE
            chunks = proj[:, hd:hd + 2 * kvd].reshape(
                batch, n_chunks, CMP_STRIDE, 2, NSA_KV_HEADS, HEAD_DIM)
            chunks = chunks.transpose(3, 0, 4, 1, 2, 5).reshape(
                2, batch, NSA_KV_HEADS, n_chunks, CMP_STRIDE * HEAD_DIM)
            comp = _compress(chunks, a_cmp_w1[layer].astype(BF16), a_cmp_w2[layer].astype(BF16),
                             a_cmp_pe[layer].reshape(2, 1, CMP_BLOCK * HEAD_DIM))
            o = _nsa_attention(proj, comp, gates, batch, seq)
            x2d = _out_project(o, a_w_out[layer].astype(BF16), x2d, seq, g1, gam[1])
        else:
            jb = layer - n_a
            if kv is None:
                kv = _project(x2d, seq, kv_norm_g[None, :], kv_mod[:, 0], kv_mod[:, 1],
                              kv_w.astype(BF16))
            q = _project(x2d, seq, gam[0], sh1, sc1, b_w_q[jb].astype(BF16), out_scale=Q_SCALE)
            o = _sb_attention(q, kv, batch, seq)
            x2d = _out_project(o, b_w_out[jb].astype(BF16), x2d, seq, g1, gam[1])
        x2d = _ffn(x2d, seq, gam[2], sh2, sc2, ffn_w_in[layer].astype(BF16),
                   ffn_w_out[layer].astype(BF16), g2, gam[3])
    return x2d.reshape(batch, seq, d)
```

```python
import functools

import jax
import jax.numpy as jnp
import numpy as np
from jax import lax
from jax.experimental import pallas as pl
from jax.experimental.pallas import tpu as pltpu

N_HEADS = 16
HEAD_DIM = 128
NSA_KV_HEADS = 4
HEADS_PER_GROUP = N_HEADS // NSA_KV_HEADS
CMP_BLOCK = 32
CMP_STRIDE = 16
CMP_HIDDEN = 256
SEL_BLOCK = 64
SEL_TOPK = 16
WINDOW = 512
D_FF = 5632
ROPE_THETA = 10000.0
EPS = 1e-6
NEG = -1e30
FORCE = 1e9
ATTN_SCALE = HEAD_DIM ** -0.5
LOG2_E = float(np.log2(np.e))
Q_SCALE = ATTN_SCALE * LOG2_E
UNSELECTED = -(2.0 ** 100)

LANES = 128
BF16 = jnp.bfloat16
F32 = jnp.float32

ROW_TILE = 1024
OUT_ROW_TILE = 512
FFN_ROW_TILE = 1024
PROJ_COL_TILE = 512
FF_TILE = 512
NSA_Q_TILE = 256
NSA_K_TILE = 512
SB_TILE = 256
SB_HEADS = 4
MOD_COL_TILE = 1024
VMEM_LIMIT = 56 * 1024 * 1024

_NT = (((1,), (1,)), ((), ()))


def _params(n_axes):
    return pltpu.CompilerParams(
        dimension_semantics=("arbitrary",) * n_axes, vmem_limit_bytes=VMEM_LIMIT)


def _dot(a, b):
    return jnp.dot(a, b, preferred_element_type=F32)


def _dot_nt(a, b):
    return lax.dot_general(a, b, _NT, preferred_element_type=F32)


def _split_bf16(x):
    hi = x.astype(BF16)
    lo = (x - hi.astype(F32)).astype(BF16)
    return hi, lo


def _rms(x, gamma):
    ms = jnp.mean(x * x, axis=-1, keepdims=True)
    return x * lax.rsqrt(ms + EPS) * gamma


def _mod_kernel(c_ref, w_ref, b_ref, o_ref):
    c = c_ref[...]
    ca = c * jax.nn.sigmoid(c)
    c_hi, c_lo = _split_bf16(ca)
    w_hi, w_lo = _split_bf16(w_ref[...])
    acc = _dot(c_hi, w_hi) + _dot(c_lo, w_hi) + _dot(c_hi, w_lo)
    o_ref[...] = acc + b_ref[...]


def _modulation(c_pad, w, b):
    n_layers, d, n = w.shape
    tn = MOD_COL_TILE
    return pl.pallas_call(
        _mod_kernel,
        grid=(n_layers, n // tn),
        in_specs=[
            pl.BlockSpec((8, d), lambda l, j: (0, 0)),
            pl.BlockSpec((None, d, tn), lambda l, j: (l, 0, j)),
            pl.BlockSpec((None, 1, tn), lambda l, j: (l, 0, j)),
        ],
        out_specs=pl.BlockSpec((None, 8, tn), lambda l, j: (l, 0, j)),
        out_shape=jax.ShapeDtypeStruct((n_layers, 8, n), F32),
        compiler_params=_params(2),
        name="modulation",
    )(c_pad, w, b)


def _normmod(x_ref, g_ref, sh_ref, sc_ref):
    y = _rms(x_ref[...], g_ref[...])
    return y * (1.0 + sc_ref[...]) + sh_ref[...]


def _proj_kernel(x_ref, g_ref, sh_ref, sc_ref, w_ref, *rest, mode, out_scale):
    if mode == "nsa":
        cos_ref, sin_ref, o_ref, h_ref = rest
    elif mode == "gate":
        b_ref, o_ref, h_ref = rest
    else:
        o_ref, h_ref = rest
    j = pl.program_id(1)

    @pl.when(j == 0)
    def _():
        h_ref[...] = _normmod(x_ref, g_ref, sh_ref, sc_ref).astype(BF16)

    acc = _dot(h_ref[...], w_ref[...])
    if mode == "plain":
        o_ref[...] = (acc * out_scale).astype(o_ref.dtype)
    elif mode == "gate":
        o_ref[...] = jax.nn.sigmoid(acc + b_ref[...])
    else:
        is_q = j < 4
        is_rope = is_q | (j == 4) | (j == 6) | (j == 8)
        scale = jnp.where(is_q, Q_SCALE, 1.0).astype(F32)
        cos = jnp.where(is_rope, cos_ref[...] * scale, 1.0)
        sin = jnp.where(is_rope, sin_ref[...] * scale, 0.0)
        for hh in range(PROJ_COL_TILE // HEAD_DIM):
            xh = acc[:, hh * HEAD_DIM:(hh + 1) * HEAD_DIM]
            rot = xh * cos + pltpu.roll(xh, HEAD_DIM // 2, axis=1) * sin
            o_ref[:, hh * HEAD_DIM:(hh + 1) * HEAD_DIM] = rot.astype(o_ref.dtype)


def _project(x2d, seq, gamma, shift, scale, w, *, mode="plain", out_scale=1.0,
             out_dtype=BF16, extra=(), tn=PROJ_COL_TILE):
    t, d = x2d.shape
    n = w.shape[1]
    tm = min(ROW_TILE, seq)
    tiles_per_seq = seq // tm
    in_specs = [
        pl.BlockSpec((tm, d), lambda i, j: (i, 0)),
        pl.BlockSpec((1, d), lambda i, j: (0, 0)),
        pl.BlockSpec((None, 1, d), lambda i, j: (i // tiles_per_seq, 0, 0)),
        pl.BlockSpec((None, 1, d), lambda i, j: (i // tiles_per_seq, 0, 0)),
        pl.BlockSpec((d, tn), lambda i, j: (0, j)),
    ]
    if mode == "nsa":
        in_specs += [pl.BlockSpec((tm, HEAD_DIM), lambda i, j: (i % tiles_per_seq, 0))] * 2
    elif mode == "gate":
        in_specs += [pl.BlockSpec((1, tn), lambda i, j: (0, j))]
    return pl.pallas_call(
        functools.partial(_proj_kernel, mode=mode, out_scale=out_scale),
        grid=(t // tm, n // tn),
        in_specs=in_specs,
        out_specs=pl.BlockSpec((tm, tn), lambda i, j: (i, j)),
        out_shape=jax.ShapeDtypeStruct((t, n), out_dtype),
        scratch_shapes=[pltpu.VMEM((tm, d), BF16)],
        compiler_params=_params(2),
        name="proj_" + mode,
    )(x2d, gamma, shift, scale, w, *extra)


def _outproj_kernel(o_ref, w_ref, x_ref, gate_ref, g_ref, out_ref):
    y = _dot(o_ref[...], w_ref[...])
    out_ref[...] = x_ref[...] + gate_ref[...] * _rms(y, g_ref[...])


def _out_project(o2d, w, x2d, seq, gate, gamma):
    t, d = x2d.shape
    k = o2d.shape[1]
    tm = min(OUT_ROW_TILE, seq)
    tiles_per_seq = seq // tm
    return pl.pallas_call(
        _outproj_kernel,
        grid=(t // tm,),
        in_specs=[
            pl.BlockSpec((tm, k), lambda i: (i, 0)),
            pl.BlockSpec((k, d), lambda i: (0, 0)),
            pl.BlockSpec((tm, d), lambda i: (i, 0)),
            pl.BlockSpec((None, 1, d), lambda i: (i // tiles_per_seq, 0, 0)),
            pl.BlockSpec((1, d), lambda i: (0, 0)),
        ],
        out_specs=pl.BlockSpec((tm, d), lambda i: (i, 0)),
        out_shape=jax.ShapeDtypeStruct((t, d), F32),
        compiler_params=_params(1),
        name="out_proj",
    )(o2d, w, x2d, gate, gamma)


def _ffn_kernel(x_ref, g_in_ref, sh_ref, sc_ref, wg_ref, wu_ref, wo_ref, gate_ref, g_out_ref,
                out_ref, h_ref):
    f = pl.program_id(1)

    @pl.when(f == 0)
    def _():
        h_ref[...] = _normmod(x_ref, g_in_ref, sh_ref, sc_ref).astype(BF16)
        out_ref[...] = jnp.zeros_like(out_ref)

    h = h_ref[...]
    gate = _dot(h, wg_ref[...])
    up = _dot(h, wu_ref[...])
    act = (gate * jax.nn.sigmoid(gate) * up).astype(BF16)
    out_ref[...] += _dot(act, wo_ref[...])

    @pl.when(f == pl.num_programs(1) - 1)
    def _():
        out_ref[...] = x_ref[...] + gate_ref[...] * _rms(out_ref[...], g_out_ref[...])


def _ffn(x2d, seq, g_in, shift, scale, w_in, w_out, gate, g_out):
    t, d = x2d.shape
    d_ff = w_out.shape[0]
    tm = min(FFN_ROW_TILE, seq)
    tf = FF_TILE
    n_f = d_ff // tf
    tiles_per_seq = seq // tm
    batch_vec = pl.BlockSpec((None, 1, d), lambda i, f: (i // tiles_per_seq, 0, 0))
    vec = pl.BlockSpec((1, d), lambda i, f: (0, 0))
    return pl.pallas_call(
        _ffn_kernel,
        grid=(t // tm, n_f),
        in_specs=[
            pl.BlockSpec((tm, d), lambda i, f: (i, 0), pipeline_mode=pl.Buffered(1)),
            vec, batch_vec, batch_vec,
            pl.BlockSpec((d, tf), lambda i, f: (0, f)),
            pl.BlockSpec((d, tf), lambda i, f: (0, n_f + f)),
            pl.BlockSpec((tf, d), lambda i, f: (f, 0)),
            batch_vec, vec,
        ],
        out_specs=pl.BlockSpec((tm, d), lambda i, f: (i, 0)),
        out_shape=jax.ShapeDtypeStruct((t, d), F32),
        scratch_shapes=[pltpu.VMEM((tm, d), BF16)],
        compiler_params=_params(2),
        name="ffn",
    )(x2d, g_in, shift, scale, w_in, w_in, w_out, gate, g_out)


def _gelu_tanh(x):
    return 0.5 * x * (1.0 + jnp.tanh(np.sqrt(2.0 / np.pi) * (x + 0.044715 * (x * x * x))))


def _compress_kernel(c_ref, w1_ref, w2_ref, pe_ref, o_ref):
    half = CMP_STRIDE * HEAD_DIM
    c = c_ref[...]
    n_chunks = c.shape[0]
    first = _dot(c, w1_ref[0:half, :])
    second = _dot(c, w1_ref[half:2 * half, :])
    pe = jnp.broadcast_to(pe_ref[...], (8, 2 * half)).astype(BF16)
    bias = _dot(pe, w1_ref[...])[0:1, :]
    hidden = first + pltpu.roll(second, n_chunks - 1, axis=0) + bias
    out = _dot(_gelu_tanh(hidden).astype(BF16), w2_ref[...])
    row = lax.broadcasted_iota(jnp.int32, out.shape, 0)
    o_ref[...] = jnp.where(row < n_chunks - 1, out, 0.0).astype(o_ref.dtype)


def _compress(chunks, w1, w2, pe):
    _, b, g, n_chunks, width = chunks.shape
    return pl.pallas_call(
        _compress_kernel,
        grid=(2, b, g),
        in_specs=[
            pl.BlockSpec((None, None, None, n_chunks, width), lambda w, bi, gi: (w, bi, gi, 0, 0)),
            pl.BlockSpec((None, 2 * width, CMP_HIDDEN), lambda w, bi, gi: (w, 0, 0)),
            pl.BlockSpec((None, CMP_HIDDEN, HEAD_DIM), lambda w, bi, gi: (w, 0, 0)),
            pl.BlockSpec((None, 1, 2 * width), lambda w, bi, gi: (w, 0, 0)),
        ],
        out_specs=pl.BlockSpec((None, None, None, n_chunks, HEAD_DIM),
                               lambda w, bi, gi: (w, bi, gi, 0, 0)),
        out_shape=jax.ShapeDtypeStruct((2, b, g, n_chunks, HEAD_DIM), BF16),
        compiler_params=_params(3),
        name="compress",
    )(chunks, w1, w2, pe)


def _nsa_kernel(q_ref, kc_ref, vc_ref, ks_ref, vs_ref, kw_ref, vw_ref, gate_ref, o_ref,
                kaug_ref, qaug_ref, m_ref, l_ref, acc_ref, side_ref, sa_ref, sb_ref):
    tq = NSA_Q_TILE
    tk = NSA_K_TILE
    r_heads = HEADS_PER_GROUP
    rows = r_heads * tq
    seq = ks_ref.shape[0]
    n_cmp = kc_ref.shape[0]
    qi = pl.program_id(2)
    t0 = qi * tq

    @pl.when(qi == 0)
    def _():
        kaug_ref[:, 0:HEAD_DIM] = ks_ref[...]
        row = lax.broadcasted_iota(jnp.int32, (seq, LANES), 0)
        lane = lax.broadcasted_iota(jnp.int32, (seq, LANES), 1)
        kaug_ref[:, HEAD_DIM:2 * HEAD_DIM] = jnp.where(
            (row // SEL_BLOCK) == lane, 1.0, 0.0).astype(BF16)

    for r in range(r_heads):
        qaug_ref[r * tq:(r + 1) * tq, 0:HEAD_DIM] = q_ref[:, r * HEAD_DIM:(r + 1) * HEAD_DIM]
    q_all = qaug_ref[:, 0:HEAD_DIM]
    t_row = t0 + (lax.broadcasted_iota(jnp.int32, (rows, 1), 0) & (tq - 1))
    gates = gate_ref[...]

    def gate_col(branch):
        return jnp.concatenate([gates[:, 3 * r + branch:3 * r + branch + 1] for r in range(r_heads)], axis=0)

    span = WINDOW + tq
    w_start = pl.multiple_of(jnp.maximum(t0 - WINDOW, 0), tq)
    s_c = _dot_nt(q_all, kc_ref[...])
    s_w = _dot_nt(q_all, kw_ref[pl.ds(w_start, span), :])

    n_idx = lax.broadcasted_iota(jnp.int32, (rows, n_cmp), 1)
    valid = (n_idx * CMP_STRIDE + (CMP_BLOCK - 1)) <= t_row
    any_valid = jnp.where(t_row >= CMP_BLOCK - 1, 1.0, 0.0).astype(F32)
    s_c = jnp.where(valid, s_c, NEG)
    e_c = jnp.exp2(s_c - jnp.max(s_c, axis=-1, keepdims=True))
    p_c = e_c / jnp.sum(e_c, axis=-1, keepdims=True) * any_valid
    o_cmp = _dot(p_c.astype(BF16), vc_ref[...])
    p_sum = p_c[0:tq]
    for r in range(1, r_heads):
        p_sum = p_sum + p_c[r * tq:(r + 1) * tq]

    kpos = w_start + lax.broadcasted_iota(jnp.int32, (rows, span), 1)
    in_window = (kpos <= t_row) & (kpos > t_row - WINDOW)
    s_w = jnp.where(in_window, s_w, NEG)
    e_w = jnp.exp2(s_w - jnp.max(s_w, axis=-1, keepdims=True))
    o_win = _dot(e_w.astype(BF16), vw_ref[pl.ds(w_start, span), :]) / jnp.sum(e_w, axis=-1, keepdims=True)
    side_ref[...] = gate_col(0) * o_cmp + gate_col(2) * o_win

    n_blocks = seq // SEL_BLOCK
    n_sel = min(SEL_TOPK, n_blocks)
    m_o = lax.broadcasted_iota(jnp.int32, (LANES, n_cmp), 0)
    n_o = lax.broadcasted_iota(jnp.int32, (LANES, n_cmp), 1)
    overlap_t = jnp.where(n_o * CMP_STRIDE < (m_o + 1) * SEL_BLOCK,
                          jnp.where(n_o * CMP_STRIDE + CMP_BLOCK > m_o * SEL_BLOCK, 1.0, 0.0),
                          0.0).astype(BF16)
    p_hi, p_lo = _split_bf16(p_sum)
    imp = (_dot_nt(overlap_t, p_hi) + _dot_nt(overlap_t, p_lo))[0:n_blocks]
    blk = lax.broadcasted_iota(jnp.int32, (n_blocks, tq), 0)
    t_s = t0 + lax.broadcasted_iota(jnp.int32, (n_blocks, tq), 1)
    cur = t_s // SEL_BLOCK
    forced = (blk == 0) | (blk == cur) | (blk == cur - 1)
    avail = blk * SEL_BLOCK <= t_s
    imp = jnp.where(forced, FORCE, jnp.where(avail, imp, -FORCE))
    sub = 8
    imp_v = [imp[v * sub:(v + 1) * sub] for v in range(n_blocks // sub)]
    blk_v = lax.broadcasted_iota(jnp.int32, (sub, tq), 0)
    rank_v = [jnp.zeros((sub, tq), F32) for _ in imp_v]
    for mp in range(n_blocks):
        other = imp[mp:mp + 1]
        for v in range(len(imp_v)):
            wins_ties = jnp.where(other >= imp_v[v], 1.0, 0.0)
            loses_ties = jnp.where(other > imp_v[v], 1.0, 0.0)
            if v * sub > mp:
                ahead = wins_ties
            elif (v + 1) * sub - 1 <= mp:
                ahead = loses_ties
            else:
                ahead = jnp.where(blk_v + v * sub > mp, wins_ties, loses_ties)
            rank_v[v] = rank_v[v] + ahead
    rank = jnp.concatenate(rank_v, axis=0)
    mask_t = jnp.where(rank < n_sel, 0.0, UNSELECTED)
    mask_t = jnp.concatenate([mask_t, jnp.zeros((LANES - n_blocks, tq), F32)], axis=0)
    mask_feat = mask_t.T.astype(BF16)
    for r in range(r_heads):
        qaug_ref[r * tq:(r + 1) * tq, HEAD_DIM:2 * HEAD_DIM] = mask_feat

    m_ref[...] = jnp.full_like(m_ref, NEG)
    l_ref[...] = jnp.zeros_like(l_ref)
    acc_ref[...] = jnp.zeros_like(acc_ref)
    def scores(j, dst_ref):
        start = pl.multiple_of(j * tk, tk)
        dst_ref[...] = _dot_nt(qaug_ref[...], kaug_ref[pl.ds(start, tk), :])

    def consume(src_ref, j, causal):
        start = pl.multiple_of(j * tk, tk)
        s = src_ref[...]
        if causal:
            kp = start + lax.broadcasted_iota(jnp.int32, (rows, tk), 1)
            s = jnp.where(kp <= t_row, s, NEG)
        m_old = m_ref[...]
        m_new = jnp.maximum(m_old, jnp.max(s, axis=-1, keepdims=True))
        alpha = jnp.exp2(m_old - m_new)
        p = jnp.exp2(s - m_new[:, 0:1])
        l_ref[...] = alpha * l_ref[...] + jnp.sum(p, axis=-1, keepdims=True)
        acc_ref[...] = alpha * acc_ref[...] + _dot(p.astype(BF16), vs_ref[pl.ds(start, tk), :])
        m_ref[...] = m_new

    n_full = t0 // tk
    n_pairs = n_full // 2
    scores(0, sa_ref)

    def pair_body(i, carry):
        j = 2 * i
        scores(j + 1, sb_ref)
        consume(sa_ref, j, False)
        scores(j + 2, sa_ref)
        consume(sb_ref, j + 1, False)
        return carry

    lax.fori_loop(0, n_pairs, pair_body, 0)
    j_next = 2 * n_pairs

    @pl.when(j_next < n_full)
    def _():
        scores(j_next + 1, sb_ref)
        consume(sa_ref, j_next, False)
        consume(sb_ref, j_next + 1, True)

    @pl.when(j_next == n_full)
    def _():
        consume(sa_ref, j_next, True)

    out = side_ref[...] + gate_col(1) * (acc_ref[...] / l_ref[...])
    for r in range(r_heads):
        o_ref[:, r * HEAD_DIM:(r + 1) * HEAD_DIM] = out[r * tq:(r + 1) * tq].astype(o_ref.dtype)


def _nsa_attention(proj, comp, gates, batch, seq):
    t = proj.shape[0]
    tq = NSA_Q_TILE
    n_q = seq // tq
    rows = HEADS_PER_GROUP * tq
    n_chunks = comp.shape[3]
    kv_cols = NSA_KV_HEADS
    q_cols = N_HEADS

    def kv_spec(which):
        base = q_cols + which * kv_cols
        return pl.BlockSpec((seq, HEAD_DIM), lambda b, g, qi: (b, base + g))

    def cmp_spec(which):
        return pl.BlockSpec((None, None, None, n_chunks, HEAD_DIM),
                            lambda b, g, qi: (which, b, g, 0, 0))

    return pl.pallas_call(
        _nsa_kernel,
        grid=(batch, NSA_KV_HEADS, n_q),
        in_specs=[
            pl.BlockSpec((tq, HEADS_PER_GROUP * HEAD_DIM), lambda b, g, qi: (b * n_q + qi, g)),
            cmp_spec(0), cmp_spec(1),
            kv_spec(2), kv_spec(3), kv_spec(4), kv_spec(5),
            pl.BlockSpec((tq, LANES), lambda b, g, qi: (b * n_q + qi, g)),
        ],
        out_specs=pl.BlockSpec((tq, HEADS_PER_GROUP * HEAD_DIM), lambda b, g, qi: (b * n_q + qi, g)),
        out_shape=jax.ShapeDtypeStruct((t, N_HEADS * HEAD_DIM), BF16),
        scratch_shapes=[
            pltpu.VMEM((seq, 2 * HEAD_DIM), BF16),
            pltpu.VMEM((rows, 2 * HEAD_DIM), BF16),
            pltpu.VMEM((rows, LANES), F32),
            pltpu.VMEM((rows, LANES), F32),
            pltpu.VMEM((rows, HEAD_DIM), F32),
            pltpu.VMEM((rows, HEAD_DIM), F32),
            pltpu.VMEM((rows, NSA_K_TILE), F32),
            pltpu.VMEM((rows, NSA_K_TILE), F32),
        ],
        compiler_params=_params(3),
        name="nsa_attention",
    )(proj, comp, comp, proj, proj, proj, proj, gates)


def _sb_kernel(q_ref, k_ref, v_ref, o_ref, carry_ref, acc_ref, za_ref, zb_ref):
    tk = SB_TILE
    qi = pl.program_id(2)
    row = lax.broadcasted_iota(jnp.int32, (tk, tk), 0)
    col = lax.broadcasted_iota(jnp.int32, (tk, tk), 1)
    causal = col < row
    not_before = jnp.where(row >= col, 1.0, 0.0).astype(BF16)
    not_before2 = jnp.concatenate([not_before, not_before], axis=0)
    carry_ref[...] = jnp.zeros_like(carry_ref)
    acc_ref[...] = jnp.zeros_like(acc_ref)
    sign_bit = jnp.uint32(0x80000000)

    heads = range(SB_HEADS)
    hs = [slice(h * HEAD_DIM, (h + 1) * HEAD_DIM) for h in heads]

    def scores(j, dst_ref):
        start = pl.multiple_of(j * tk, tk)
        for h in heads:
            dst_ref[h] = _dot_nt(q_ref[:, hs[h]], k_ref[pl.ds(start, tk), hs[h]])

    def consume(src_ref, j, diagonal):
        start = pl.multiple_of(j * tk, tk)
        z = [src_ref[h] for h in heads]
        neg_abs = [pltpu.bitcast(pltpu.bitcast(z[h], jnp.uint32) | sign_bit, F32) for h in heads]
        sp = [jnp.maximum(z[h], 0.0) + jnp.log(1.0 + jnp.exp2(neg_abs[h])) * LOG2_E for h in heads]
        if diagonal:
            sp = [jnp.where(causal, sp[h], 0.0) for h in heads]
        split = [jnp.concatenate(_split_bf16(sp[h]), axis=1) for h in heads]
        suffix = [_dot(split[h], not_before2) for h in heads]
        carry = [carry_ref[h] for h in heads]
        a = [jnp.exp2(z[h] - suffix[h] - jnp.concatenate([carry[h]] * (tk // LANES), axis=1))
             for h in heads]
        if diagonal:
            a = [jnp.where(causal, a[h], 0.0) for h in heads]
        for h in heads:
            acc_ref[h] += _dot(a[h].astype(BF16), v_ref[pl.ds(start, tk), hs[h]])
            carry_ref[h] = carry[h] + suffix[h][:, 0:1]

    scores(qi, za_ref)
    scores(jnp.maximum(qi - 1, 0), zb_ref)
    consume(za_ref, qi, True)
    n_pairs = jnp.maximum(qi - 1, 0) // 2

    def pair_body(i, c):
        j = qi - 1 - 2 * i
        scores(j - 1, za_ref)
        consume(zb_ref, j, False)
        scores(j - 2, zb_ref)
        consume(za_ref, j - 1, False)
        return c

    lax.fori_loop(0, n_pairs, pair_body, 0)
    j_next = qi - 1 - 2 * n_pairs

    @pl.when(j_next == 1)
    def _():
        scores(0, za_ref)
        consume(zb_ref, 1, False)
        consume(za_ref, 0, False)

    @pl.when(j_next == 0)
    def _():
        consume(zb_ref, 0, False)

    for h in range(SB_HEADS):
        o_ref[:, h * HEAD_DIM:(h + 1) * HEAD_DIM] = acc_ref[h].astype(o_ref.dtype)


def _sb_attention(q, kv, batch, seq):
    t = q.shape[0]
    tq = SB_TILE
    n_q = seq // tq
    width = SB_HEADS * HEAD_DIM
    n_hb = N_HEADS // SB_HEADS
    return pl.pallas_call(
        _sb_kernel,
        grid=(batch, n_hb, n_q),
        in_specs=[
            pl.BlockSpec((tq, width), lambda b, h, qi: (b * n_q + qi, h)),
            pl.BlockSpec((seq, width), lambda b, h, qi: (b, h)),
            pl.BlockSpec((seq, width), lambda b, h, qi: (b, n_hb + h)),
        ],
        out_specs=pl.BlockSpec((tq, width), lambda b, h, qi: (b * n_q + qi, h)),
        out_shape=jax.ShapeDtypeStruct((t, N_HEADS * HEAD_DIM), BF16),
        scratch_shapes=[pltpu.VMEM((SB_HEADS, tq, LANES), F32),
                        pltpu.VMEM((SB_HEADS, tq, HEAD_DIM), F32),
                        pltpu.VMEM((SB_HEADS, tq, SB_TILE), F32),
                        pltpu.VMEM((SB_HEADS, tq, SB_TILE), F32)],
        compiler_params=_params(3),
        name="sb_attention",
    )(q, kv, kv)


def _rope_tables(seq):
    half = HEAD_DIM // 2
    inv = ROPE_THETA ** (-jnp.arange(half, dtype=F32) / half)
    ang = jnp.arange(seq, dtype=F32)[:, None] * inv[None, :]
    cos, sin = jnp.cos(ang), jnp.sin(ang)
    return jnp.concatenate([cos, cos], axis=-1), jnp.concatenate([-sin, sin], axis=-1)


def kernel(x, c, mod_w, mod_b, norm_g, ffn_w_in, ffn_w_out, a_w_in, a_gate_b, a_cmp_pe, a_cmp_w1,
           a_cmp_w2, a_w_out, b_w_q, b_w_out, kv_norm_g, kv_mod_w, kv_mod_b, kv_w):
    batch, seq, d = x.shape
    depth = mod_w.shape[0]
    n_a = a_w_in.shape[0]
    t = batch * seq
    hd = N_HEADS * HEAD_DIM
    kvd = NSA_KV_HEADS * HEAD_DIM
    x2d = x.reshape(t, d)

    c_pad = jnp.pad(c, ((0, 8 - batch), (0, 0)))
    mod = _modulation(c_pad, mod_w, mod_b[:, None, :])[:, :batch]
    mod = mod.reshape(depth, batch, 6, 1, d)
    kv_mod = _modulation(c_pad, kv_mod_w[None], kv_mod_b[None, None, :])[0, :batch]
    kv_mod = kv_mod.reshape(batch, 2, 1, d)
    cos, sin = _rope_tables(seq)

    kv = None
    for layer in range(depth):
        sh1, sc1, g1, sh2, sc2, g2 = (mod[layer, :, i] for i in range(6))
        gam = norm_g[layer][:, None, :]
        if layer < n_a:
            w_in = a_w_in[layer]
            n_main = hd + 6 * kvd
            proj = _project(x2d, seq, gam[0], sh1, sc1, w_in[:, :n_main].astype(BF16),
                            mode="nsa", extra=(cos, sin))
            n_gate = 3 * HEADS_PER_GROUP
            w_gate = w_in[:, n_main:].reshape(d, NSA_KV_HEADS, n_gate)
            w_gate = jnp.pad(w_gate, ((0, 0), (0, 0), (0, LANES - n_gate))).reshape(d, NSA_KV_HEADS * LANES)
            b_gate = jnp.pad(a_gate_b[layer].reshape(NSA_KV_HEADS, n_gate),
                             ((0, 0), (0, LANES - n_gate))).reshape(1, NSA_KV_HEADS * LANES)
            gates = _project(x2d, seq, gam[0], sh1, sc1, w_gate.astype(BF16), mode="gate",
                             out_dtype=F32, extra=(b_gate,), tn=NSA_KV_HEADS * LANES)
            n_chunks = seq // CMP_STRIDE
            chunks = proj[:, hd:hd + 2 * kvd].reshape(
                batch, n_chunks, CMP_STRIDE, 2, NSA_KV_HEADS, HEAD_DIM)
            chunks = chunks.transpose(3, 0, 4, 1, 2, 5).reshape(
                2, batch, NSA_KV_HEADS, n_chunks, CMP_STRIDE * HEAD_DIM)
            comp = _compress(chunks, a_cmp_w1[layer].astype(BF16), a_cmp_w2[layer].astype(BF16),
                             a_cmp_pe[layer].reshape(2, 1, CMP_BLOCK * HEAD_DIM))
            o = _nsa_attention(proj, comp, gates, batch, seq)
            x2d = _out_project(o, a_w_out[layer].astype(BF16), x2d, seq, g1, gam[1])
        else:
            jb = layer - n_a
            if kv is None:
                kv = _project(x2d, seq, kv_norm_g[None, :], kv_mod[:, 0], kv_mod[:, 1],
                              kv_w.astype(BF16))
            q = _project(x2d, seq, gam[0], sh1, sc1, b_w_q[jb].astype(BF16), out_scale=Q_SCALE)
            o = _sb_attention(q, kv, batch, seq)
            x2d = _out_project(o, b_w_out[jb].astype(BF16), x2d, seq, g1, gam[1])
        x2d = _ffn(x2d, seq, gam[2], sh2, sc2, ffn_w_in[layer].astype(BF16),
                   ffn_w_out[layer].astype(BF16), g2, gam[3])
    return x2d.reshape(batch, seq, d)
```

```python
import functools

import jax
import jax.numpy as jnp
import numpy as np
from jax import lax
from jax.experimental import pallas as pl
from jax.experimental.pallas import tpu as pltpu

N_HEADS = 16
HEAD_DIM = 128
NSA_KV_HEADS = 4
HEADS_PER_GROUP = N_HEADS // NSA_KV_HEADS
CMP_BLOCK = 32
CMP_STRIDE = 16
CMP_HIDDEN = 256
SEL_BLOCK = 64
SEL_TOPK = 16
WINDOW = 512
D_FF = 5632
ROPE_THETA = 10000.0
EPS = 1e-6
NEG = -1e30
FORCE = 1e9
ATTN_SCALE = HEAD_DIM ** -0.5
LOG2_E = float(np.log2(np.e))
Q_SCALE = ATTN_SCALE * LOG2_E
UNSELECTED = -(2.0 ** 100)

LANES = 128
BF16 = jnp.bfloat16
F32 = jnp.float32

ROW_TILE = 1024
OUT_ROW_TILE = 512
FFN_ROW_TILE = 1024
PROJ_COL_TILE = 512
FF_TILE = 512
NSA_Q_TILE = 256
NSA_K_TILE = 512
SB_TILE = 256
SB_HEADS = 4
MOD_COL_TILE = 1024
VMEM_LIMIT = 56 * 1024 * 1024

_NT = (((1,), (1,)), ((), ()))


def _params(n_axes):
    return pltpu.CompilerParams(
        dimension_semantics=("arbitrary",) * n_axes, vmem_limit_bytes=VMEM_LIMIT)


def _dot(a, b):
    return jnp.dot(a, b, preferred_element_type=F32)


def _dot_nt(a, b):
    return lax.dot_general(a, b, _NT, preferred_element_type=F32)


def _split_bf16(x):
    hi = x.astype(BF16)
    lo = (x - hi.astype(F32)).astype(BF16)
    return hi, lo


def _rms(x, gamma):
    ms = jnp.mean(x * x, axis=-1, keepdims=True)
    return x * lax.rsqrt(ms + EPS) * gamma


def _mod_kernel(c_ref, w_ref, b_ref, o_ref):
    c = c_ref[...]
    ca = c * jax.nn.sigmoid(c)
    c_hi, c_lo = _split_bf16(ca)
    w_hi, w_lo = _split_bf16(w_ref[...])
    acc = _dot(c_hi, w_hi) + _dot(c_lo, w_hi) + _dot(c_hi, w_lo)
    o_ref[...] = acc + b_ref[...]


def _modulation(c_pad, w, b):
    n_layers, d, n = w.shape
    tn = MOD_COL_TILE
    assert n % tn == 0, (n, tn)
    return pl.pallas_call(
        _mod_kernel,
        grid=(n_layers, n // tn),
        in_specs=[
            pl.BlockSpec((8, d), lambda l, j: (0, 0)),
            pl.BlockSpec((None, d, tn), lambda l, j: (l, 0, j)),
            pl.BlockSpec((None, 1, tn), lambda l, j: (l, 0, j)),
        ],
        out_specs=pl.BlockSpec((None, 8, tn), lambda l, j: (l, 0, j)),
        out_shape=jax.ShapeDtypeStruct((n_layers, 8, n), F32),
        compiler_params=_params(2),
        name="modulation",
    )(c_pad, w, b)


def _normmod(x_ref, g_ref, sh_ref, sc_ref):
    y = _rms(x_ref[...], g_ref[...])
    return y * (1.0 + sc_ref[...]) + sh_ref[...]


def _proj_kernel(x_ref, g_ref, sh_ref, sc_ref, w_ref, *rest, mode, out_scale):
    if mode == "nsa":
        cos_ref, sin_ref, o_ref, h_ref = rest
    elif mode == "gate":
        b_ref, o_ref, h_ref = rest
    else:
        o_ref, h_ref = rest
    j = pl.program_id(1)

    @pl.when(j == 0)
    def _():
        h_ref[...] = _normmod(x_ref, g_ref, sh_ref, sc_ref).astype(BF16)

    acc = _dot(h_ref[...], w_ref[...])
    if mode == "plain":
        o_ref[...] = (acc * out_scale).astype(o_ref.dtype)
    elif mode == "gate":
        o_ref[...] = jax.nn.sigmoid(acc + b_ref[...])
    else:
        is_q = j < 4
        is_rope = is_q | (j == 4) | (j == 6) | (j == 8)
        scale = jnp.where(is_q, Q_SCALE, 1.0).astype(F32)
        cos = jnp.where(is_rope, cos_ref[...] * scale, 1.0)
        sin = jnp.where(is_rope, sin_ref[...] * scale, 0.0)
        for hh in range(PROJ_COL_TILE // HEAD_DIM):
            xh = acc[:, hh * HEAD_DIM:(hh + 1) * HEAD_DIM]
            rot = xh * cos + pltpu.roll(xh, HEAD_DIM // 2, axis=1) * sin
            o_ref[:, hh * HEAD_DIM:(hh + 1) * HEAD_DIM] = rot.astype(o_ref.dtype)


def _project(x2d, seq, gamma, shift, scale, w, *, mode="plain", out_scale=1.0,
             out_dtype=BF16, extra=(), tn=PROJ_COL_TILE):
    t, d = x2d.shape
    n = w.shape[1]
    tm = min(ROW_TILE, seq)
    assert seq % tm == 0 and n % tn == 0, (seq, tm, n, tn)
    tiles_per_seq = seq // tm
    in_specs = [
        pl.BlockSpec((tm, d), lambda i, j: (i, 0)),
        pl.BlockSpec((1, d), lambda i, j: (0, 0)),
        pl.BlockSpec((None, 1, d), lambda i, j: (i // tiles_per_seq, 0, 0)),
        pl.BlockSpec((None, 1, d), lambda i, j: (i // tiles_per_seq, 0, 0)),
        pl.BlockSpec((d, tn), lambda i, j: (0, j)),
    ]
    if mode == "nsa":
        in_specs += [pl.BlockSpec((tm, HEAD_DIM), lambda i, j: (i % tiles_per_seq, 0))] * 2
    elif mode == "gate":
        in_specs += [pl.BlockSpec((1, tn), lambda i, j: (0, j))]
    return pl.pallas_call(
        functools.partial(_proj_kernel, mode=mode, out_scale=out_scale),
        grid=(t // tm, n // tn),
        in_specs=in_specs,
        out_specs=pl.BlockSpec((tm, tn), lambda i, j: (i, j)),
        out_shape=jax.ShapeDtypeStruct((t, n), out_dtype),
        scratch_shapes=[pltpu.VMEM((tm, d), BF16)],
        compiler_params=_params(2),
        name="proj_" + mode,
    )(x2d, gamma, shift, scale, w, *extra)


def _outproj_kernel(o_ref, w_ref, x_ref, gate_ref, g_ref, out_ref):
    y = _dot(o_ref[...], w_ref[...])
    out_ref[...] = x_ref[...] + gate_ref[...] * _rms(y, g_ref[...])


def _out_project(o2d, w, x2d, seq, gate, gamma):
    t, d = x2d.shape
    k = o2d.shape[1]
    tm = min(OUT_ROW_TILE, seq)
    assert seq % tm == 0, (seq, tm)
    tiles_per_seq = seq // tm
    return pl.pallas_call(
        _outproj_kernel,
        grid=(t // tm,),
        in_specs=[
            pl.BlockSpec((tm, k), lambda i: (i, 0)),
            pl.BlockSpec((k, d), lambda i: (0, 0)),
            pl.BlockSpec((tm, d), lambda i: (i, 0)),
            pl.BlockSpec((None, 1, d), lambda i: (i // tiles_per_seq, 0, 0)),
            pl.BlockSpec((1, d), lambda i: (0, 0)),
        ],
        out_specs=pl.BlockSpec((tm, d), lambda i: (i, 0)),
        out_shape=jax.ShapeDtypeStruct((t, d), F32),
        compiler_params=_params(1),
        name="out_proj",
    )(o2d, w, x2d, gate, gamma)


def _ffn_kernel(x_ref, g_in_ref, sh_ref, sc_ref, wg_ref, wu_ref, wo_ref, gate_ref, g_out_ref,
                out_ref, h_ref):
    f = pl.program_id(1)

    @pl.when(f == 0)
    def _():
        h_ref[...] = _normmod(x_ref, g_in_ref, sh_ref, sc_ref).astype(BF16)
        out_ref[...] = jnp.zeros_like(out_ref)

    h = h_ref[...]
    gate = _dot(h, wg_ref[...])
    up = _dot(h, wu_ref[...])
    act = (gate * jax.nn.sigmoid(gate) * up).astype(BF16)
    out_ref[...] += _dot(act, wo_ref[...])

    @pl.when(f == pl.num_programs(1) - 1)
    def _():
        out_ref[...] = x_ref[...] + gate_ref[...] * _rms(out_ref[...], g_out_ref[...])


def _ffn(x2d, seq, g_in, shift, scale, w_in, w_out, gate, g_out, tm=FFN_ROW_TILE, tf=FF_TILE,
         x_buffers=1):
    t, d = x2d.shape
    d_ff = w_out.shape[0]
    tm = min(tm, seq)
    assert d_ff % tf == 0 and seq % tm == 0, (d_ff, tf, seq, tm)
    n_f = d_ff // tf
    tiles_per_seq = seq // tm
    batch_vec = pl.BlockSpec((None, 1, d), lambda i, f: (i // tiles_per_seq, 0, 0))
    vec = pl.BlockSpec((1, d), lambda i, f: (0, 0))
    return pl.pallas_call(
        _ffn_kernel,
        grid=(t // tm, n_f),
        in_specs=[
            pl.BlockSpec((tm, d), lambda i, f: (i, 0), pipeline_mode=pl.Buffered(x_buffers)),
            vec, batch_vec, batch_vec,
            pl.BlockSpec((d, tf), lambda i, f: (0, f)),
            pl.BlockSpec((d, tf), lambda i, f: (0, n_f + f)),
            pl.BlockSpec((tf, d), lambda i, f: (f, 0)),
            batch_vec, vec,
        ],
        out_specs=pl.BlockSpec((tm, d), lambda i, f: (i, 0)),
        out_shape=jax.ShapeDtypeStruct((t, d), F32),
        scratch_shapes=[pltpu.VMEM((tm, d), BF16)],
        compiler_params=_params(2),
        name="ffn",
    )(x2d, g_in, shift, scale, w_in, w_in, w_out, gate, g_out)


def _gelu_tanh(x):
    return 0.5 * x * (1.0 + jnp.tanh(np.sqrt(2.0 / np.pi) * (x + 0.044715 * (x * x * x))))


def _compress_kernel(c_ref, w1_ref, w2_ref, pe_ref, o_ref):
    half = CMP_STRIDE * HEAD_DIM
    c = c_ref[...]
    n_chunks = c.shape[0]
    first = _dot(c, w1_ref[0:half, :])
    second = _dot(c, w1_ref[half:2 * half, :])
    pe = jnp.broadcast_to(pe_ref[...], (8, 2 * half)).astype(BF16)
    bias = _dot(pe, w1_ref[...])[0:1, :]
    hidden = first + pltpu.roll(second, n_chunks - 1, axis=0) + bias
    out = _dot(_gelu_tanh(hidden).astype(BF16), w2_ref[...])
    row = lax.broadcasted_iota(jnp.int32, out.shape, 0)
    o_ref[...] = jnp.where(row < n_chunks - 1, out, 0.0).astype(o_ref.dtype)


def _compress(chunks, w1, w2, pe):
    _, b, g, n_chunks, width = chunks.shape
    return pl.pallas_call(
        _compress_kernel,
        grid=(2, b, g),
        in_specs=[
            pl.BlockSpec((None, None, None, n_chunks, width), lambda w, bi, gi: (w, bi, gi, 0, 0)),
            pl.BlockSpec((None, 2 * width, CMP_HIDDEN), lambda w, bi, gi: (w, 0, 0)),
            pl.BlockSpec((None, CMP_HIDDEN, HEAD_DIM), lambda w, bi, gi: (w, 0, 0)),
            pl.BlockSpec((None, 1, 2 * width), lambda w, bi, gi: (w, 0, 0)),
        ],
        out_specs=pl.BlockSpec((None, None, None, n_chunks, HEAD_DIM),
                               lambda w, bi, gi: (w, bi, gi, 0, 0)),
        out_shape=jax.ShapeDtypeStruct((2, b, g, n_chunks, HEAD_DIM), BF16),
        compiler_params=_params(3),
        name="compress",
    )(chunks, w1, w2, pe)


def _nsa_kernel(q_ref, kc_ref, vc_ref, ks_ref, vs_ref, kw_ref, vw_ref, gate_ref, o_ref,
                kaug_ref, qaug_ref, m_ref, l_ref, acc_ref, side_ref, sa_ref, sb_ref):
    tq = q_ref.shape[0]
    tk = sa_ref.shape[1]
    r_heads = HEADS_PER_GROUP
    rows = r_heads * tq
    seq = ks_ref.shape[0]
    n_cmp = kc_ref.shape[0]
    qi = pl.program_id(2)
    t0 = qi * tq

    @pl.when(qi == 0)
    def _():
        kaug_ref[:, 0:HEAD_DIM] = ks_ref[...]
        row = lax.broadcasted_iota(jnp.int32, (seq, LANES), 0)
        lane = lax.broadcasted_iota(jnp.int32, (seq, LANES), 1)
        kaug_ref[:, HEAD_DIM:2 * HEAD_DIM] = jnp.where(
            (row // SEL_BLOCK) == lane, 1.0, 0.0).astype(BF16)

    for r in range(r_heads):
        qaug_ref[r * tq:(r + 1) * tq, 0:HEAD_DIM] = q_ref[:, r * HEAD_DIM:(r + 1) * HEAD_DIM]
    q_all = qaug_ref[:, 0:HEAD_DIM]
    t_row = t0 + (lax.broadcasted_iota(jnp.int32, (rows, 1), 0) & (tq - 1))
    gates = gate_ref[...]

    def gate_col(branch):
        return jnp.concatenate([gates[:, 3 * r + branch:3 * r + branch + 1] for r in range(r_heads)], axis=0)

    span = WINDOW + tq
    w_start = pl.multiple_of(jnp.maximum(t0 - WINDOW, 0), tq)
    s_c = _dot_nt(q_all, kc_ref[...])
    s_w = _dot_nt(q_all, kw_ref[pl.ds(w_start, span), :])

    n_idx = lax.broadcasted_iota(jnp.int32, (rows, n_cmp), 1)
    valid = (n_idx * CMP_STRIDE + (CMP_BLOCK - 1)) <= t_row
    any_valid = jnp.where(t_row >= CMP_BLOCK - 1, 1.0, 0.0).astype(F32)
    s_c = jnp.where(valid, s_c, NEG)
    e_c = jnp.exp2(s_c - jnp.max(s_c, axis=-1, keepdims=True))
    p_c = e_c / jnp.sum(e_c, axis=-1, keepdims=True) * any_valid
    o_cmp = _dot(p_c.astype(BF16), vc_ref[...])
    p_sum = p_c[0:tq]
    for r in range(1, r_heads):
        p_sum = p_sum + p_c[r * tq:(r + 1) * tq]

    kpos = w_start + lax.broadcasted_iota(jnp.int32, (rows, span), 1)
    in_window = (kpos <= t_row) & (kpos > t_row - WINDOW)
    s_w = jnp.where(in_window, s_w, NEG)
    e_w = jnp.exp2(s_w - jnp.max(s_w, axis=-1, keepdims=True))
    o_win = _dot(e_w.astype(BF16), vw_ref[pl.ds(w_start, span), :]) / jnp.sum(e_w, axis=-1, keepdims=True)
    side_ref[...] = gate_col(0) * o_cmp + gate_col(2) * o_win

    n_blocks = seq // SEL_BLOCK
    n_sel = min(SEL_TOPK, n_blocks)
    m_o = lax.broadcasted_iota(jnp.int32, (LANES, n_cmp), 0)
    n_o = lax.broadcasted_iota(jnp.int32, (LANES, n_cmp), 1)
    overlap_t = jnp.where(n_o * CMP_STRIDE < (m_o + 1) * SEL_BLOCK,
                          jnp.where(n_o * CMP_STRIDE + CMP_BLOCK > m_o * SEL_BLOCK, 1.0, 0.0),
                          0.0).astype(BF16)
    p_hi, p_lo = _split_bf16(p_sum)
    imp = (_dot_nt(overlap_t, p_hi) + _dot_nt(overlap_t, p_lo))[0:n_blocks]
    blk = lax.broadcasted_iota(jnp.int32, (n_blocks, tq), 0)
    t_s = t0 + lax.broadcasted_iota(jnp.int32, (n_blocks, tq), 1)
    cur = t_s // SEL_BLOCK
    forced = (blk == 0) | (blk == cur) | (blk == cur - 1)
    avail = blk * SEL_BLOCK <= t_s
    imp = jnp.where(forced, FORCE, jnp.where(avail, imp, -FORCE))
    sub = 8
    imp_v = [imp[v * sub:(v + 1) * sub] for v in range(n_blocks // sub)]
    blk_v = lax.broadcasted_iota(jnp.int32, (sub, tq), 0)
    rank_v = [jnp.zeros((sub, tq), F32) for _ in imp_v]
    for mp in range(n_blocks):
        other = imp[mp:mp + 1]
        for v in range(len(imp_v)):
            wins_ties = jnp.where(other >= imp_v[v], 1.0, 0.0)
            loses_ties = jnp.where(other > imp_v[v], 1.0, 0.0)
            if v * sub > mp:
                ahead = wins_ties
            elif (v + 1) * sub - 1 <= mp:
                ahead = loses_ties
            else:
                ahead = jnp.where(blk_v + v * sub > mp, wins_ties, loses_ties)
            rank_v[v] = rank_v[v] + ahead
    rank = jnp.concatenate(rank_v, axis=0)
    mask_t = jnp.where(rank < n_sel, 0.0, UNSELECTED)
    mask_t = jnp.concatenate([mask_t, jnp.zeros((LANES - n_blocks, tq), F32)], axis=0)
    mask_feat = mask_t.T.astype(BF16)
    for r in range(r_heads):
        qaug_ref[r * tq:(r + 1) * tq, HEAD_DIM:2 * HEAD_DIM] = mask_feat

    m_ref[...] = jnp.full_like(m_ref, NEG)
    l_ref[...] = jnp.zeros_like(l_ref)
    acc_ref[...] = jnp.zeros_like(acc_ref)
    def scores(j, dst_ref):
        start = pl.multiple_of(j * tk, tk)
        dst_ref[...] = _dot_nt(qaug_ref[...], kaug_ref[pl.ds(start, tk), :])

    def consume(src_ref, j, causal):
        start = pl.multiple_of(j * tk, tk)
        s = src_ref[...]
        if causal:
            kp = start + lax.broadcasted_iota(jnp.int32, (rows, tk), 1)
            s = jnp.where(kp <= t_row, s, NEG)
        m_old = m_ref[...]
        m_new = jnp.maximum(m_old, jnp.max(s, axis=-1, keepdims=True))
        alpha = jnp.exp2(m_old - m_new)
        p = jnp.exp2(s - m_new[:, 0:1])
        l_ref[...] = alpha * l_ref[...] + jnp.sum(p, axis=-1, keepdims=True)
        acc_ref[...] = alpha * acc_ref[...] + _dot(p.astype(BF16), vs_ref[pl.ds(start, tk), :])
        m_ref[...] = m_new

    n_full = t0 // tk
    n_pairs = n_full // 2
    scores(0, sa_ref)

    def pair_body(i, carry):
        j = 2 * i
        scores(j + 1, sb_ref)
        consume(sa_ref, j, False)
        scores(j + 2, sa_ref)
        consume(sb_ref, j + 1, False)
        return carry

    lax.fori_loop(0, n_pairs, pair_body, 0)
    j_next = 2 * n_pairs

    @pl.when(j_next < n_full)
    def _():
        scores(j_next + 1, sb_ref)
        consume(sa_ref, j_next, False)
        consume(sb_ref, j_next + 1, True)

    @pl.when(j_next == n_full)
    def _():
        consume(sa_ref, j_next, True)

    out = side_ref[...] + gate_col(1) * (acc_ref[...] / l_ref[...])
    for r in range(r_heads):
        o_ref[:, r * HEAD_DIM:(r + 1) * HEAD_DIM] = out[r * tq:(r + 1) * tq].astype(o_ref.dtype)


def _nsa_attention(proj, comp, gates, batch, seq, tq=NSA_Q_TILE, tk=NSA_K_TILE):
    t = proj.shape[0]
    assert seq % tk == 0 and tk % tq == 0 and seq % SEL_BLOCK == 0, (seq, tq, tk)
    n_q = seq // tq
    rows = HEADS_PER_GROUP * tq
    n_chunks = comp.shape[3]
    kv_cols = NSA_KV_HEADS
    q_cols = N_HEADS

    def kv_spec(which):
        base = q_cols + which * kv_cols
        return pl.BlockSpec((seq, HEAD_DIM), lambda b, g, qi: (b, base + g))

    def cmp_spec(which):
        return pl.BlockSpec((None, None, None, n_chunks, HEAD_DIM),
                            lambda b, g, qi: (which, b, g, 0, 0))

    return pl.pallas_call(
        _nsa_kernel,
        grid=(batch, NSA_KV_HEADS, n_q),
        in_specs=[
            pl.BlockSpec((tq, HEADS_PER_GROUP * HEAD_DIM), lambda b, g, qi: (b * n_q + qi, g)),
            cmp_spec(0), cmp_spec(1),
            kv_spec(2), kv_spec(3), kv_spec(4), kv_spec(5),
            pl.BlockSpec((tq, LANES), lambda b, g, qi: (b * n_q + qi, g)),
        ],
        out_specs=pl.BlockSpec((tq, HEADS_PER_GROUP * HEAD_DIM), lambda b, g, qi: (b * n_q + qi, g)),
        out_shape=jax.ShapeDtypeStruct((t, N_HEADS * HEAD_DIM), BF16),
        scratch_shapes=[
            pltpu.VMEM((seq, 2 * HEAD_DIM), BF16),
            pltpu.VMEM((rows, 2 * HEAD_DIM), BF16),
            pltpu.VMEM((rows, LANES), F32),
            pltpu.VMEM((rows, LANES), F32),
            pltpu.VMEM((rows, HEAD_DIM), F32),
            pltpu.VMEM((rows, HEAD_DIM), F32),
            pltpu.VMEM((rows, tk), F32),
            pltpu.VMEM((rows, tk), F32),
        ],
        compiler_params=_params(3),
        name="nsa_attention",
    )(proj, comp, comp, proj, proj, proj, proj, gates)


def _sb_kernel(q_ref, k_ref, v_ref, o_ref, carry_ref, acc_ref, za_ref, zb_ref):
    tk = SB_TILE
    qi = pl.program_id(2)
    row = lax.broadcasted_iota(jnp.int32, (tk, tk), 0)
    col = lax.broadcasted_iota(jnp.int32, (tk, tk), 1)
    causal = col < row
    not_before = jnp.where(row >= col, 1.0, 0.0).astype(BF16)
    not_before2 = jnp.concatenate([not_before, not_before], axis=0)
    carry_ref[...] = jnp.zeros_like(carry_ref)
    acc_ref[...] = jnp.zeros_like(acc_ref)
    sign_bit = jnp.uint32(0x80000000)

    heads = range(carry_ref.shape[0])
    hs = [slice(h * HEAD_DIM, (h + 1) * HEAD_DIM) for h in heads]

    def scores(j, dst_ref):
        start = pl.multiple_of(j * tk, tk)
        for h in heads:
            dst_ref[h] = _dot_nt(q_ref[:, hs[h]], k_ref[pl.ds(start, tk), hs[h]])

    def consume(src_ref, j, diagonal):
        start = pl.multiple_of(j * tk, tk)
        z = [src_ref[h] for h in heads]
        neg_abs = [pltpu.bitcast(pltpu.bitcast(z[h], jnp.uint32) | sign_bit, F32) for h in heads]
        sp = [jnp.maximum(z[h], 0.0) + jnp.log(1.0 + jnp.exp2(neg_abs[h])) * LOG2_E for h in heads]
        if diagonal:
            sp = [jnp.where(causal, sp[h], 0.0) for h in heads]
        split = [jnp.concatenate(_split_bf16(sp[h]), axis=1) for h in heads]
        suffix = [_dot(split[h], not_before2) for h in heads]
        carry = [carry_ref[h] for h in heads]
        a = [jnp.exp2(z[h] - suffix[h] - jnp.concatenate([carry[h]] * (tk // LANES), axis=1))
             for h in heads]
        if diagonal:
            a = [jnp.where(causal, a[h], 0.0) for h in heads]
        for h in heads:
            acc_ref[h] += _dot(a[h].astype(BF16), v_ref[pl.ds(start, tk), hs[h]])
            carry_ref[h] = carry[h] + suffix[h][:, 0:1]

    scores(qi, za_ref)
    scores(jnp.maximum(qi - 1, 0), zb_ref)
    consume(za_ref, qi, True)
    n_pairs = jnp.maximum(qi - 1, 0) // 2

    def pair_body(i, c):
        j = qi - 1 - 2 * i
        scores(j - 1, za_ref)
        consume(zb_ref, j, False)
        scores(j - 2, zb_ref)
        consume(za_ref, j - 1, False)
        return c

    lax.fori_loop(0, n_pairs, pair_body, 0)
    j_next = qi - 1 - 2 * n_pairs

    @pl.when(j_next == 1)
    def _():
        scores(0, za_ref)
        consume(zb_ref, 1, False)
        consume(za_ref, 0, False)

    @pl.when(j_next == 0)
    def _():
        consume(zb_ref, 0, False)

    for h in heads:
        o_ref[:, h * HEAD_DIM:(h + 1) * HEAD_DIM] = acc_ref[h].astype(o_ref.dtype)


def _sb_attention(q, kv, batch, seq, n_heads=SB_HEADS):
    t = q.shape[0]
    tq = SB_TILE
    assert seq % tq == 0 and N_HEADS % n_heads == 0, (seq, tq, n_heads)
    n_q = seq // tq
    width = n_heads * HEAD_DIM
    n_hb = N_HEADS // n_heads
    return pl.pallas_call(
        _sb_kernel,
        grid=(batch, n_hb, n_q),
        in_specs=[
            pl.BlockSpec((tq, width), lambda b, h, qi: (b * n_q + qi, h)),
            pl.BlockSpec((seq, width), lambda b, h, qi: (b, h)),
            pl.BlockSpec((seq, width), lambda b, h, qi: (b, n_hb + h)),
        ],
        out_specs=pl.BlockSpec((tq, width), lambda b, h, qi: (b * n_q + qi, h)),
        out_shape=jax.ShapeDtypeStruct((t, N_HEADS * HEAD_DIM), BF16),
        scratch_shapes=[pltpu.VMEM((n_heads, tq, LANES), F32),
                        pltpu.VMEM((n_heads, tq, HEAD_DIM), F32),
                        pltpu.VMEM((n_heads, tq, SB_TILE), F32),
                        pltpu.VMEM((n_heads, tq, SB_TILE), F32)],
        compiler_params=_params(3),
        name="sb_attention",
    )(q, kv, kv)


def _rope_tables(seq):
    half = HEAD_DIM // 2
    inv = ROPE_THETA ** (-jnp.arange(half, dtype=F32) / half)
    ang = jnp.arange(seq, dtype=F32)[:, None] * inv[None, :]
    cos, sin = jnp.cos(ang), jnp.sin(ang)
    return jnp.concatenate([cos, cos], axis=-1), jnp.concatenate([-sin, sin], axis=-1)


def kernel(x, c, mod_w, mod_b, norm_g, ffn_w_in, ffn_w_out, a_w_in, a_gate_b, a_cmp_pe, a_cmp_w1,
           a_cmp_w2, a_w_out, b_w_q, b_w_out, kv_norm_g, kv_mod_w, kv_mod_b, kv_w):
    batch, seq, d = x.shape
    depth = mod_w.shape[0]
    n_a = a_w_in.shape[0]
    t = batch * seq
    hd = N_HEADS * HEAD_DIM
    kvd = NSA_KV_HEADS * HEAD_DIM
    x2d = x.reshape(t, d)

    c_pad = jnp.pad(c, ((0, 8 - batch), (0, 0)))
    mod = _modulation(c_pad, mod_w, mod_b[:, None, :])[:, :batch]
    mod = mod.reshape(depth, batch, 6, 1, d)
    kv_mod = _modulation(c_pad, kv_mod_w[None], kv_mod_b[None, None, :])[0, :batch]
    kv_mod = kv_mod.reshape(batch, 2, 1, d)
    cos, sin = _rope_tables(seq)

    kv = None
    for layer in range(depth):
        sh1, sc1, g1, sh2, sc2, g2 = (mod[layer, :, i] for i in range(6))
        gam = norm_g[layer][:, None, :]
        if layer < n_a:
            w_in = a_w_in[layer]
            n_main = hd + 6 * kvd
            proj = _project(x2d, seq, gam[0], sh1, sc1, w_in[:, :n_main].astype(BF16),
                            mode="nsa", extra=(cos, sin))
            n_gate = 3 * HEADS_PER_GROUP
            w_gate = w_in[:, n_main:].reshape(d, NSA_KV_HEADS, n_gate)
            w_gate = jnp.pad(w_gate, ((0, 0), (0, 0), (0, LANES - n_gate))).reshape(d, NSA_KV_HEADS * LANES)
            b_gate = jnp.pad(a_gate_b[layer].reshape(NSA_KV_HEADS, n_gate),
                             ((0, 0), (0, LANES - n_gate))).reshape(1, NSA_KV_HEADS * LANES)
            gates = _project(x2d, seq, gam[0], sh1, sc1, w_gate.astype(BF16), mode="gate",
                             out_dtype=F32, extra=(b_gate,), tn=NSA_KV_HEADS * LANES)
            n_chunks = seq // CMP_STRIDE
            chunks = proj[:, hd:hd + 2 * kvd].reshape(
                batch, n_chunks, CMP_STRIDE, 2, NSA_KV_HEADS, HEAD_DIM)
            chunks = chunks.transpose(3, 0, 4, 1, 2, 5).reshape(
                2, batch, NSA_KV_HEADS, n_chunks, CMP_STRIDE * HEAD_DIM)
            comp = _compress(chunks, a_cmp_w1[layer].astype(BF16), a_cmp_w2[layer].astype(BF16),
                             a_cmp_pe[layer].reshape(2, 1, CMP_BLOCK * HEAD_DIM))
            o = _nsa_attention(proj, comp, gates, batch, seq, tk=(512, 256)[layer])
            x2d = _out_project(o, a_w_out[layer].astype(BF16), x2d, seq, g1, gam[1])
        else:
            jb = layer - n_a
            if kv is None:
                kv = _project(x2d, seq, kv_norm_g[None, :], kv_mod[:, 0], kv_mod[:, 1],
                              kv_w.astype(BF16))
            q = _project(x2d, seq, gam[0], sh1, sc1, b_w_q[jb].astype(BF16), out_scale=Q_SCALE,
                         tn=(512, 1024)[jb])
            o = _sb_attention(q, kv, batch, seq, n_heads=(4, 8)[jb])
            x2d = _out_project(o, b_w_out[jb].astype(BF16), x2d, seq, g1, gam[1])
        x2d = _ffn(x2d, seq, gam[2], sh2, sc2, ffn_w_in[layer].astype(BF16),
                   ffn_w_out[layer].astype(BF16), g2, gam[3],
                   *((512, 512, 2), (512, 256, 2), (1024, 256, 2), (1024, 512, 1))[layer])
    return x2d.reshape(batch, seq, d)
```

```python
import functools

import jax
import jax.numpy as jnp
import numpy as np
from jax import lax
from jax.experimental import pallas as pl
from jax.experimental.pallas import tpu as pltpu

N_HEADS = 16
HEAD_DIM = 128
NSA_KV_HEADS = 4
HEADS_PER_GROUP = N_HEADS // NSA_KV_HEADS
CMP_BLOCK = 32
CMP_STRIDE = 16
CMP_HIDDEN = 256
SEL_BLOCK = 64
SEL_TOPK = 16
WINDOW = 512
D_FF = 5632
ROPE_THETA = 10000.0
EPS = 1e-6
NEG = -1e30
FORCE = 1e9
ATTN_SCALE = HEAD_DIM ** -0.5
LOG2_E = float(np.log2(np.e))
Q_SCALE = ATTN_SCALE * LOG2_E
UNSELECTED = -(2.0 ** 100)

LANES = 128
BF16 = jnp.bfloat16
F32 = jnp.float32

ROW_TILE = 1024
OUT_ROW_TILE = 512
FFN_ROW_TILE = 1024
PROJ_COL_TILE = 1024
FF_TILE = 512
NSA_Q_TILE = 256
NSA_K_TILE = 512
SB_TILE = 256
SB_HEADS = 8
MOD_COL_TILE = 1024
VMEM_LIMIT = 56 * 1024 * 1024

_NT = (((1,), (1,)), ((), ()))


def _params(n_axes, vmem_limit=VMEM_LIMIT):
    return pltpu.CompilerParams(
        dimension_semantics=("arbitrary",) * n_axes, vmem_limit_bytes=vmem_limit)


def _dot(a, b):
    return jnp.dot(a, b, preferred_element_type=F32)


def _dot_nt(a, b):
    return lax.dot_general(a, b, _NT, preferred_element_type=F32)


def _split_bf16(x):
    hi = x.astype(BF16)
    lo = (x - hi.astype(F32)).astype(BF16)
    return hi, lo


def _rms(x, gamma):
    ms = jnp.mean(x * x, axis=-1, keepdims=True)
    return x * lax.rsqrt(ms + EPS) * gamma


def _mod_kernel(c_ref, w_ref, b_ref, o_ref):
    c = c_ref[...]
    ca = c * jax.nn.sigmoid(c)
    c_hi, c_lo = _split_bf16(ca)
    w_hi, w_lo = _split_bf16(w_ref[...])
    acc = _dot(c_hi, w_hi) + _dot(c_lo, w_hi) + _dot(c_hi, w_lo)
    o_ref[...] = acc + b_ref[...]


def _modulation(c_pad, w, b):
    n_layers, d, n = w.shape
    tn = MOD_COL_TILE
    assert n % tn == 0, (n, tn)
    return pl.pallas_call(
        _mod_kernel,
        grid=(n_layers, n // tn),
        in_specs=[
            pl.BlockSpec((8, d), lambda l, j: (0, 0)),
            pl.BlockSpec((None, d, tn), lambda l, j: (l, 0, j)),
            pl.BlockSpec((None, 1, tn), lambda l, j: (l, 0, j)),
        ],
        out_specs=pl.BlockSpec((None, 8, tn), lambda l, j: (l, 0, j)),
        out_shape=jax.ShapeDtypeStruct((n_layers, 8, n), F32),
        compiler_params=_params(2),
        name="modulation",
    )(c_pad, w, b)


def _normmod(x_ref, g_ref, sh_ref, sc_ref):
    y = _rms(x_ref[...], g_ref[...])
    return y * (1.0 + sc_ref[...]) + sh_ref[...]


def _proj_kernel(x_ref, g_ref, sh_ref, sc_ref, w_ref, *rest, mode, out_scale):
    if mode == "nsa":
        cos_ref, sin_ref, o_ref, chunk_ref, h_ref, relayout_ref = rest
    elif mode == "gate":
        b_ref, o_ref, h_ref = rest
    else:
        o_ref, h_ref = rest
    j = pl.program_id(1)

    @pl.when(j == 0)
    def _():
        h_ref[...] = _normmod(x_ref, g_ref, sh_ref, sc_ref).astype(BF16)

    acc = _dot(h_ref[...], w_ref[...])
    if mode == "plain":
        o_ref[...] = (acc * out_scale).astype(o_ref.dtype)
    elif mode == "gate":
        o_ref[...] = jax.nn.sigmoid(acc + b_ref[...])
    else:
        half = NSA_KV_HEADS
        is_q = j < 2
        scale = jnp.where(is_q, Q_SCALE, 1.0).astype(F32)
        cos_k = cos_ref[...] * scale
        sin_k = sin_ref[...] * scale
        cos_v = jnp.where(is_q, cos_k, 1.0)
        sin_v = jnp.where(is_q, sin_k, 0.0)

        def rotate(hh, cos, sin):
            xh = acc[:, hh * HEAD_DIM:(hh + 1) * HEAD_DIM]
            return xh * cos + pltpu.roll(xh, HEAD_DIM // 2, axis=1) * sin

        for hh in range(2 * half):
            rot = rotate(hh, cos_k, sin_k) if hh < half else rotate(hh, cos_v, sin_v)
            o_ref[:, hh * HEAD_DIM:(hh + 1) * HEAD_DIM] = rot.astype(o_ref.dtype)

        @pl.when(j == 2)
        def _():
            n_chunk_rows = chunk_ref.shape[2]
            for hh in range(2 * half):
                if hh < half:
                    relayout_ref[...] = rotate(hh, cos_ref[...], sin_ref[...])
                else:
                    relayout_ref[...] = acc[:, hh * HEAD_DIM:(hh + 1) * HEAD_DIM]
                for l in range(CMP_STRIDE):
                    rows_l = relayout_ref[pl.ds(l, n_chunk_rows, stride=CMP_STRIDE), :]
                    chunk_ref[hh // half, hh % half, :, l * HEAD_DIM:(l + 1) * HEAD_DIM] = (
                        rows_l.astype(chunk_ref.dtype))


def _project(x2d, seq, gamma, shift, scale, w, *, mode="plain", out_scale=1.0,
             out_dtype=BF16, extra=(), tn=PROJ_COL_TILE):
    t, d = x2d.shape
    n = w.shape[1]
    tm = min(ROW_TILE, seq)
    assert seq % tm == 0 and n % tn == 0, (seq, tm, n, tn)
    tiles_per_seq = seq // tm
    in_specs = [
        pl.BlockSpec((tm, d), lambda i, j: (i, 0)),
        pl.BlockSpec((1, d), lambda i, j: (0, 0)),
        pl.BlockSpec((None, 1, d), lambda i, j: (i // tiles_per_seq, 0, 0)),
        pl.BlockSpec((None, 1, d), lambda i, j: (i // tiles_per_seq, 0, 0)),
        pl.BlockSpec((d, tn), lambda i, j: (0, j)),
    ]
    out_specs = pl.BlockSpec((tm, tn), lambda i, j: (i, j))
    out_shape = jax.ShapeDtypeStruct((t, n), out_dtype)
    scratch_shapes = [pltpu.VMEM((tm, d), BF16)]
    if mode == "nsa":
        assert tn == 2 * NSA_KV_HEADS * HEAD_DIM and tm % CMP_STRIDE == 0, (tn, tm)
        in_specs += [pl.BlockSpec((tm, HEAD_DIM), lambda i, j: (i % tiles_per_seq, 0))] * 2
        chunk_rows = tm // CMP_STRIDE
        width = CMP_STRIDE * HEAD_DIM
        out_specs = [out_specs,
                     pl.BlockSpec((2, None, NSA_KV_HEADS, chunk_rows, width),
                                  lambda i, j: (0, i // tiles_per_seq, 0, i % tiles_per_seq, 0))]
        out_shape = [out_shape,
                     jax.ShapeDtypeStruct((2, t // seq, NSA_KV_HEADS, seq // CMP_STRIDE, width), BF16)]
        scratch_shapes += [pltpu.VMEM((tm, HEAD_DIM), F32)]
    elif mode == "gate":
        in_specs += [pl.BlockSpec((1, tn), lambda i, j: (0, j))]
    return pl.pallas_call(
        functools.partial(_proj_kernel, mode=mode, out_scale=out_scale),
        grid=(t // tm, n // tn),
        in_specs=in_specs,
        out_specs=out_specs,
        out_shape=out_shape,
        scratch_shapes=scratch_shapes,
        compiler_params=_params(2),
        name="proj_" + mode,
    )(x2d, gamma, shift, scale, w, *extra)


def _outproj_kernel(o_ref, w_ref, x_ref, gate_ref, g_ref, out_ref):
    y = _dot(o_ref[...], w_ref[...])
    out_ref[...] = x_ref[...] + gate_ref[...] * _rms(y, g_ref[...])


def _out_project(o2d, w, x2d, seq, gate, gamma):
    t, d = x2d.shape
    k = o2d.shape[1]
    tm = min(OUT_ROW_TILE, seq)
    assert seq % tm == 0, (seq, tm)
    tiles_per_seq = seq // tm
    return pl.pallas_call(
        _outproj_kernel,
        grid=(t // tm,),
        in_specs=[
            pl.BlockSpec((tm, k), lambda i: (i, 0)),
            pl.BlockSpec((k, d), lambda i: (0, 0)),
            pl.BlockSpec((tm, d), lambda i: (i, 0)),
            pl.BlockSpec((None, 1, d), lambda i: (i // tiles_per_seq, 0, 0)),
            pl.BlockSpec((1, d), lambda i: (0, 0)),
        ],
        out_specs=pl.BlockSpec((tm, d), lambda i: (i, 0)),
        out_shape=jax.ShapeDtypeStruct((t, d), F32),
        compiler_params=_params(1),
        name="out_proj",
    )(o2d, w, x2d, gate, gamma)


def _ffn_kernel(x_ref, g_in_ref, sh_ref, sc_ref, wg_ref, wu_ref, wo_ref, gate_ref, g_out_ref,
                out_ref, h_ref):
    f = pl.program_id(1)

    @pl.when(f == 0)
    def _():
        h_ref[...] = _normmod(x_ref, g_in_ref, sh_ref, sc_ref).astype(BF16)
        out_ref[...] = jnp.zeros_like(out_ref)

    h = h_ref[...]
    gate = _dot(h, wg_ref[...])
    up = _dot(h, wu_ref[...])
    act = (gate * jax.nn.sigmoid(gate) * up).astype(BF16)
    out_ref[...] += _dot(act, wo_ref[...])

    @pl.when(f == pl.num_programs(1) - 1)
    def _():
        out_ref[...] = x_ref[...] + gate_ref[...] * _rms(out_ref[...], g_out_ref[...])


def _ffn(x2d, seq, g_in, shift, scale, w_in, w_out, gate, g_out, tm=FFN_ROW_TILE, tf=FF_TILE,
         x_buffers=2, vmem_limit=VMEM_LIMIT):
    t, d = x2d.shape
    d_ff = w_out.shape[0]
    tm = min(tm, seq)
    assert d_ff % tf == 0 and seq % tm == 0, (d_ff, tf, seq, tm)
    n_f = d_ff // tf
    tiles_per_seq = seq // tm
    batch_vec = pl.BlockSpec((None, 1, d), lambda i, f: (i // tiles_per_seq, 0, 0))
    vec = pl.BlockSpec((1, d), lambda i, f: (0, 0))
    return pl.pallas_call(
        _ffn_kernel,
        grid=(t // tm, n_f),
        in_specs=[
            pl.BlockSpec((tm, d), lambda i, f: (i, 0), pipeline_mode=pl.Buffered(x_buffers)),
            vec, batch_vec, batch_vec,
            pl.BlockSpec((d, tf), lambda i, f: (0, f)),
            pl.BlockSpec((d, tf), lambda i, f: (0, n_f + f)),
            pl.BlockSpec((tf, d), lambda i, f: (f, 0)),
            batch_vec, vec,
        ],
        out_specs=pl.BlockSpec((tm, d), lambda i, f: (i, 0)),
        out_shape=jax.ShapeDtypeStruct((t, d), F32),
        scratch_shapes=[pltpu.VMEM((tm, d), BF16)],
        compiler_params=_params(2, vmem_limit),
        name="ffn",
    )(x2d, g_in, shift, scale, w_in, w_in, w_out, gate, g_out)


def _gelu_tanh(x):
    return 0.5 * x * (1.0 + jnp.tanh(np.sqrt(2.0 / np.pi) * (x + 0.044715 * (x * x * x))))


def _compress_kernel(c_ref, w1_ref, w2_ref, pe_ref, o_ref):
    half = CMP_STRIDE * HEAD_DIM
    c = c_ref[...]
    n_chunks = c.shape[0]
    first = _dot(c, w1_ref[0:half, :])
    second = _dot(c, w1_ref[half:2 * half, :])
    pe = jnp.broadcast_to(pe_ref[...], (8, 2 * half)).astype(BF16)
    bias = _dot(pe, w1_ref[...])[0:1, :]
    hidden = first + pltpu.roll(second, n_chunks - 1, axis=0) + bias
    out = _dot(_gelu_tanh(hidden).astype(BF16), w2_ref[...])
    row = lax.broadcasted_iota(jnp.int32, out.shape, 0)
    o_ref[...] = jnp.where(row < n_chunks - 1, out, 0.0).astype(o_ref.dtype)


def _compress(chunks, w1, w2, pe):
    _, b, g, n_chunks, width = chunks.shape
    return pl.pallas_call(
        _compress_kernel,
        grid=(2, b, g),
        in_specs=[
            pl.BlockSpec((None, None, None, n_chunks, width), lambda w, bi, gi: (w, bi, gi, 0, 0)),
            pl.BlockSpec((None, 2 * width, CMP_HIDDEN), lambda w, bi, gi: (w, 0, 0)),
            pl.BlockSpec((None, CMP_HIDDEN, HEAD_DIM), lambda w, bi, gi: (w, 0, 0)),
            pl.BlockSpec((None, 1, 2 * width), lambda w, bi, gi: (w, 0, 0)),
        ],
        out_specs=pl.BlockSpec((None, None, None, n_chunks, HEAD_DIM),
                               lambda w, bi, gi: (w, bi, gi, 0, 0)),
        out_shape=jax.ShapeDtypeStruct((2, b, g, n_chunks, HEAD_DIM), BF16),
        compiler_params=_params(3),
        name="compress",
    )(chunks, w1, w2, pe)


def _nsa_kernel(q_ref, kc_ref, vc_ref, ks_ref, vs_ref, kw_ref, vw_ref, gate_ref, o_ref,
                kaug_ref, qaug_ref, m_ref, l_ref, acc_ref, side_ref, sa_ref, sb_ref):
    tq = q_ref.shape[0]
    tk = sa_ref.shape[1]
    r_heads = HEADS_PER_GROUP
    rows = r_heads * tq
    seq = ks_ref.shape[0]
    n_cmp = kc_ref.shape[0]
    qi = pl.program_id(2)
    t0 = qi * tq

    @pl.when(qi == 0)
    def _():
        kaug_ref[:, 0:HEAD_DIM] = ks_ref[...]
        row = lax.broadcasted_iota(jnp.int32, (seq, LANES), 0)
        lane = lax.broadcasted_iota(jnp.int32, (seq, LANES), 1)
        kaug_ref[:, HEAD_DIM:2 * HEAD_DIM] = jnp.where(
            (row // SEL_BLOCK) == lane, 1.0, 0.0).astype(BF16)

    for r in range(r_heads):
        qaug_ref[r * tq:(r + 1) * tq, 0:HEAD_DIM] = q_ref[:, r * HEAD_DIM:(r + 1) * HEAD_DIM]
    q_all = qaug_ref[:, 0:HEAD_DIM]
    t_row = t0 + (lax.broadcasted_iota(jnp.int32, (rows, 1), 0) & (tq - 1))
    gates = gate_ref[...]

    def gate_col(branch):
        return jnp.concatenate([gates[:, 3 * r + branch:3 * r + branch + 1] for r in range(r_heads)], axis=0)

    span = WINDOW + tq
    w_start = pl.multiple_of(jnp.maximum(t0 - WINDOW, 0), tq)
    s_c = _dot_nt(q_all, kc_ref[...])
    s_w = _dot_nt(q_all, kw_ref[pl.ds(w_start, span), :])

    n_idx = lax.broadcasted_iota(jnp.int32, (rows, n_cmp), 1)
    valid = (n_idx * CMP_STRIDE + (CMP_BLOCK - 1)) <= t_row
    any_valid = jnp.where(t_row >= CMP_BLOCK - 1, 1.0, 0.0).astype(F32)
    s_c = jnp.where(valid, s_c, NEG)
    e_c = jnp.exp2(s_c - jnp.max(s_c, axis=-1, keepdims=True))
    p_c = e_c / jnp.sum(e_c, axis=-1, keepdims=True) * any_valid
    o_cmp = _dot(p_c.astype(BF16), vc_ref[...])
    p_sum = p_c[0:tq]
    for r in range(1, r_heads):
        p_sum = p_sum + p_c[r * tq:(r + 1) * tq]

    kpos = w_start + lax.broadcasted_iota(jnp.int32, (rows, span), 1)
    in_window = (kpos <= t_row) & (kpos > t_row - WINDOW)
    s_w = jnp.where(in_window, s_w, NEG)
    e_w = jnp.exp2(s_w - jnp.max(s_w, axis=-1, keepdims=True))
    o_win = _dot(e_w.astype(BF16), vw_ref[pl.ds(w_start, span), :]) / jnp.sum(e_w, axis=-1, keepdims=True)
    side_ref[...] = gate_col(0) * o_cmp + gate_col(2) * o_win

    n_blocks = seq // SEL_BLOCK
    n_sel = min(SEL_TOPK, n_blocks)
    m_o = lax.broadcasted_iota(jnp.int32, (LANES, n_cmp), 0)
    n_o = lax.broadcasted_iota(jnp.int32, (LANES, n_cmp), 1)
    overlap_t = jnp.where(n_o * CMP_STRIDE < (m_o + 1) * SEL_BLOCK,
                          jnp.where(n_o * CMP_STRIDE + CMP_BLOCK > m_o * SEL_BLOCK, 1.0, 0.0),
                          0.0).astype(BF16)
    p_hi, p_lo = _split_bf16(p_sum)
    imp = (_dot_nt(overlap_t, p_hi) + _dot_nt(overlap_t, p_lo))[0:n_blocks]
    blk = lax.broadcasted_iota(jnp.int32, (n_blocks, tq), 0)
    t_s = t0 + lax.broadcasted_iota(jnp.int32, (n_blocks, tq), 1)
    cur = t_s // SEL_BLOCK
    forced = (blk == 0) | (blk == cur) | (blk == cur - 1)
    avail = blk * SEL_BLOCK <= t_s
    imp = jnp.where(forced, FORCE, jnp.where(avail, imp, -FORCE))
    sub = 8
    imp_v = [imp[v * sub:(v + 1) * sub] for v in range(n_blocks // sub)]
    blk_v = lax.broadcasted_iota(jnp.int32, (sub, tq), 0)
    rank_v = [jnp.zeros((sub, tq), F32) for _ in imp_v]
    for mp in range(n_blocks):
        other = imp[mp:mp + 1]
        for v in range(len(imp_v)):
            wins_ties = jnp.where(other >= imp_v[v], 1.0, 0.0)
            loses_ties = jnp.where(other > imp_v[v], 1.0, 0.0)
            if v * sub > mp:
                ahead = wins_ties
            elif (v + 1) * sub - 1 <= mp:
                ahead = loses_ties
            else:
                ahead = jnp.where(blk_v + v * sub > mp, wins_ties, loses_ties)
            rank_v[v] = rank_v[v] + ahead
    rank = jnp.concatenate(rank_v, axis=0)
    mask_t = jnp.where(rank < n_sel, 0.0, UNSELECTED)
    mask_t = jnp.concatenate([mask_t, jnp.zeros((LANES - n_blocks, tq), F32)], axis=0)
    mask_feat = mask_t.T.astype(BF16)
    for r in range(r_heads):
        qaug_ref[r * tq:(r + 1) * tq, HEAD_DIM:2 * HEAD_DIM] = mask_feat

    m_ref[...] = jnp.full_like(m_ref, NEG)
    l_ref[...] = jnp.zeros_like(l_ref)
    acc_ref[...] = jnp.zeros_like(acc_ref)
    def scores(j, dst_ref):
        start = pl.multiple_of(j * tk, tk)
        dst_ref[...] = _dot_nt(qaug_ref[...], kaug_ref[pl.ds(start, tk), :])

    def consume(src_ref, j, causal):
        start = pl.multiple_of(j * tk, tk)
        s = src_ref[...]
        if causal:
            kp = start + lax.broadcasted_iota(jnp.int32, (rows, tk), 1)
            s = jnp.where(kp <= t_row, s, NEG)
        m_old = m_ref[...]
        m_new = jnp.maximum(m_old, jnp.max(s, axis=-1, keepdims=True))
        alpha = jnp.exp2(m_old - m_new)
        p = jnp.exp2(s - m_new[:, 0:1])
        l_ref[...] = alpha * l_ref[...] + jnp.sum(p, axis=-1, keepdims=True)
        acc_ref[...] = alpha * acc_ref[...] + _dot(p.astype(BF16), vs_ref[pl.ds(start, tk), :])
        m_ref[...] = m_new

    n_full = t0 // tk
    n_pairs = n_full // 2
    scores(0, sa_ref)

    def pair_body(i, carry):
        j = 2 * i
        scores(j + 1, sb_ref)
        consume(sa_ref, j, False)
        scores(j + 2, sa_ref)
        consume(sb_ref, j + 1, False)
        return carry

    lax.fori_loop(0, n_pairs, pair_body, 0)
    j_next = 2 * n_pairs

    @pl.when(j_next < n_full)
    def _():
        scores(j_next + 1, sb_ref)
        consume(sa_ref, j_next, False)
        consume(sb_ref, j_next + 1, True)

    @pl.when(j_next == n_full)
    def _():
        consume(sa_ref, j_next, True)

    out = side_ref[...] + gate_col(1) * (acc_ref[...] / l_ref[...])
    for r in range(r_heads):
        o_ref[:, r * HEAD_DIM:(r + 1) * HEAD_DIM] = out[r * tq:(r + 1) * tq].astype(o_ref.dtype)


def _nsa_attention(proj, comp, gates, batch, seq, tq=NSA_Q_TILE, tk=NSA_K_TILE):
    t = proj.shape[0]
    assert seq % tk == 0 and tk % tq == 0 and seq % SEL_BLOCK == 0, (seq, tq, tk)
    n_q = seq // tq
    rows = HEADS_PER_GROUP * tq
    n_chunks = comp.shape[3]
    kv_cols = NSA_KV_HEADS
    q_cols = N_HEADS

    def kv_spec(which):
        base = q_cols + which * kv_cols
        return pl.BlockSpec((seq, HEAD_DIM), lambda b, g, qi: (b, base + g))

    def cmp_spec(which):
        return pl.BlockSpec((None, None, None, n_chunks, HEAD_DIM),
                            lambda b, g, qi: (which, b, g, 0, 0))

    return pl.pallas_call(
        _nsa_kernel,
        grid=(batch, NSA_KV_HEADS, n_q),
        in_specs=[
            pl.BlockSpec((tq, HEADS_PER_GROUP * HEAD_DIM), lambda b, g, qi: (b * n_q + qi, g)),
            cmp_spec(0), cmp_spec(1),
            kv_spec(2), kv_spec(3), kv_spec(4), kv_spec(5),
            pl.BlockSpec((tq, LANES), lambda b, g, qi: (b * n_q + qi, g)),
        ],
        out_specs=pl.BlockSpec((tq, HEADS_PER_GROUP * HEAD_DIM), lambda b, g, qi: (b * n_q + qi, g)),
        out_shape=jax.ShapeDtypeStruct((t, N_HEADS * HEAD_DIM), BF16),
        scratch_shapes=[
            pltpu.VMEM((seq, 2 * HEAD_DIM), BF16),
            pltpu.VMEM((rows, 2 * HEAD_DIM), BF16),
            pltpu.VMEM((rows, LANES), F32),
            pltpu.VMEM((rows, LANES), F32),
            pltpu.VMEM((rows, HEAD_DIM), F32),
            pltpu.VMEM((rows, HEAD_DIM), F32),
            pltpu.VMEM((rows, tk), F32),
            pltpu.VMEM((rows, tk), F32),
        ],
        compiler_params=_params(3),
        name="nsa_attention",
    )(proj, comp, comp, proj, proj, proj, proj, gates)


def _sb_kernel(q_ref, k_ref, v_ref, o_ref, carry_ref, acc_ref, za_ref, zb_ref):
    tk = SB_TILE
    qi = pl.program_id(2)
    row = lax.broadcasted_iota(jnp.int32, (tk, tk), 0)
    col = lax.broadcasted_iota(jnp.int32, (tk, tk), 1)
    causal = col < row
    not_before = jnp.where(row >= col, 1.0, 0.0).astype(BF16)
    not_before2 = jnp.concatenate([not_before, not_before], axis=0)
    carry_ref[...] = jnp.zeros_like(carry_ref)
    acc_ref[...] = jnp.zeros_like(acc_ref)
    sign_bit = jnp.uint32(0x80000000)

    heads = range(carry_ref.shape[0])
    hs = [slice(h * HEAD_DIM, (h + 1) * HEAD_DIM) for h in heads]

    def scores(j, dst_ref):
        start = pl.multiple_of(j * tk, tk)
        for h in heads:
            dst_ref[h] = _dot_nt(q_ref[:, hs[h]], k_ref[pl.ds(start, tk), hs[h]])

    def consume(src_ref, j, diagonal):
        start = pl.multiple_of(j * tk, tk)
        z = [src_ref[h] for h in heads]
        neg_abs = [pltpu.bitcast(pltpu.bitcast(z[h], jnp.uint32) | sign_bit, F32) for h in heads]
        sp = [jnp.maximum(z[h], 0.0) + jnp.log(1.0 + jnp.exp2(neg_abs[h])) * LOG2_E for h in heads]
        if diagonal:
            sp = [jnp.where(causal, sp[h], 0.0) for h in heads]
        split = [jnp.concatenate(_split_bf16(sp[h]), axis=1) for h in heads]
        suffix = [_dot(split[h], not_before2) for h in heads]
        carry = [carry_ref[h] for h in heads]
        a = [jnp.exp2(z[h] - suffix[h] - jnp.concatenate([carry[h]] * (tk // LANES), axis=1))
             for h in heads]
        if diagonal:
            a = [jnp.where(causal, a[h], 0.0) for h in heads]
        for h in heads:
            acc_ref[h] += _dot(a[h].astype(BF16), v_ref[pl.ds(start, tk), hs[h]])
            carry_ref[h] = carry[h] + suffix[h][:, 0:1]

    scores(qi, za_ref)
    scores(jnp.maximum(qi - 1, 0), zb_ref)
    consume(za_ref, qi, True)
    n_pairs = jnp.maximum(qi - 1, 0) // 2

    def pair_body(i, c):
        j = qi - 1 - 2 * i
        scores(j - 1, za_ref)
        consume(zb_ref, j, False)
        scores(j - 2, zb_ref)
        consume(za_ref, j - 1, False)
        return c

    lax.fori_loop(0, n_pairs, pair_body, 0)
    j_next = qi - 1 - 2 * n_pairs

    @pl.when(j_next == 1)
    def _():
        scores(0, za_ref)
        consume(zb_ref, 1, False)
        consume(za_ref, 0, False)

    @pl.when(j_next == 0)
    def _():
        consume(zb_ref, 0, False)

    for h in heads:
        o_ref[:, h * HEAD_DIM:(h + 1) * HEAD_DIM] = acc_ref[h].astype(o_ref.dtype)


def _sb_attention(q, kv, batch, seq, n_heads=SB_HEADS):
    t = q.shape[0]
    tq = SB_TILE
    assert seq % tq == 0 and N_HEADS % n_heads == 0, (seq, tq, n_heads)
    n_q = seq // tq
    width = n_heads * HEAD_DIM
    n_hb = N_HEADS // n_heads
    return pl.pallas_call(
        _sb_kernel,
        grid=(batch, n_hb, n_q),
        in_specs=[
            pl.BlockSpec((tq, width), lambda b, h, qi: (b * n_q + qi, h)),
            pl.BlockSpec((seq, width), lambda b, h, qi: (b, h)),
            pl.BlockSpec((seq, width), lambda b, h, qi: (b, n_hb + h)),
        ],
        out_specs=pl.BlockSpec((tq, width), lambda b, h, qi: (b * n_q + qi, h)),
        out_shape=jax.ShapeDtypeStruct((t, N_HEADS * HEAD_DIM), BF16),
        scratch_shapes=[pltpu.VMEM((n_heads, tq, LANES), F32),
                        pltpu.VMEM((n_heads, tq, HEAD_DIM), F32),
                        pltpu.VMEM((n_heads, tq, SB_TILE), F32),
                        pltpu.VMEM((n_heads, tq, SB_TILE), F32)],
        compiler_params=_params(3),
        name="sb_attention",
    )(q, kv, kv)


def _rope_tables(seq):
    half = HEAD_DIM // 2
    inv = ROPE_THETA ** (-jnp.arange(half, dtype=F32) / half)
    ang = jnp.arange(seq, dtype=F32)[:, None] * inv[None, :]
    cos, sin = jnp.cos(ang), jnp.sin(ang)
    return jnp.concatenate([cos, cos], axis=-1), jnp.concatenate([-sin, sin], axis=-1)


def kernel(x, c, mod_w, mod_b, norm_g, ffn_w_in, ffn_w_out, a_w_in, a_gate_b, a_cmp_pe, a_cmp_w1,
           a_cmp_w2, a_w_out, b_w_q, b_w_out, kv_norm_g, kv_mod_w, kv_mod_b, kv_w):
    batch, seq, d = x.shape
    depth = mod_w.shape[0]
    n_a = a_w_in.shape[0]
    t = batch * seq
    hd = N_HEADS * HEAD_DIM
    kvd = NSA_KV_HEADS * HEAD_DIM
    x2d = x.reshape(t, d)

    c_pad = jnp.pad(c, ((0, 8 - batch), (0, 0)))
    mod = _modulation(c_pad, mod_w, mod_b[:, None, :])[:, :batch]
    mod = mod.reshape(depth, batch, 6, 1, d)
    kv_mod = _modulation(c_pad, kv_mod_w[None], kv_mod_b[None, None, :])[0, :batch]
    kv_mod = kv_mod.reshape(batch, 2, 1, d)
    cos, sin = _rope_tables(seq)

    kv = None
    for layer in range(depth):
        sh1, sc1, g1, sh2, sc2, g2 = (mod[layer, :, i] for i in range(6))
        gam = norm_g[layer][:, None, :]
        if layer < n_a:
            w_in = a_w_in[layer]
            n_main = hd + 6 * kvd
            proj, chunks = _project(x2d, seq, gam[0], sh1, sc1, w_in[:, :n_main].astype(BF16),
                                    mode="nsa", extra=(cos, sin), tn=2 * kvd)
            n_gate = 3 * HEADS_PER_GROUP
            w_gate = w_in[:, n_main:].reshape(d, NSA_KV_HEADS, n_gate)
            w_gate = jnp.pad(w_gate, ((0, 0), (0, 0), (0, LANES - n_gate))).reshape(d, NSA_KV_HEADS * LANES)
            b_gate = jnp.pad(a_gate_b[layer].reshape(NSA_KV_HEADS, n_gate),
                             ((0, 0), (0, LANES - n_gate))).reshape(1, NSA_KV_HEADS * LANES)
            gates = _project(x2d, seq, gam[0], sh1, sc1, w_gate.astype(BF16), mode="gate",
                             out_dtype=F32, extra=(b_gate,), tn=NSA_KV_HEADS * LANES)
            comp = _compress(chunks, a_cmp_w1[layer].astype(BF16), a_cmp_w2[layer].astype(BF16),
                             a_cmp_pe[layer].reshape(2, 1, CMP_BLOCK * HEAD_DIM))
            o = _nsa_attention(proj, comp, gates, batch, seq)
            x2d = _out_project(o, a_w_out[layer].astype(BF16), x2d, seq, g1, gam[1])
        else:
            jb = layer - n_a
            if kv is None:
                kv = _project(x2d, seq, kv_norm_g[None, :], kv_mod[:, 0], kv_mod[:, 1],
                              kv_w.astype(BF16))
            q = _project(x2d, seq, gam[0], sh1, sc1, b_w_q[jb].astype(BF16), out_scale=Q_SCALE)
            o = _sb_attention(q, kv, batch, seq)
            x2d = _out_project(o, b_w_out[jb].astype(BF16), x2d, seq, g1, gam[1])
        x2d = _ffn(x2d, seq, gam[2], sh2, sc2, ffn_w_in[layer].astype(BF16),
                   ffn_w_out[layer].astype(BF16), g2, gam[3],
                   *((512, 512, 2, VMEM_LIMIT), (1024, 512, 2, 62 * 1024 * 1024),
                     (512, 512, 2, VMEM_LIMIT), (512, 512, 2, VMEM_LIMIT))[layer])
    return x2d.reshape(batch, seq, d)
```

```python
import functools

import jax
import jax.numpy as jnp
import numpy as np
from jax import lax
from jax.experimental import pallas as pl
from jax.experimental.pallas import tpu as pltpu

N_HEADS = 16
HEAD_DIM = 128
NSA_KV_HEADS = 4
HEADS_PER_GROUP = N_HEADS // NSA_KV_HEADS
CMP_BLOCK = 32
CMP_STRIDE = 16
CMP_HIDDEN = 256
SEL_BLOCK = 64
SEL_TOPK = 16
WINDOW = 512
D_FF = 5632
ROPE_THETA = 10000.0
EPS = 1e-6
NEG = -1e30
FORCE = 1e9
ATTN_SCALE = HEAD_DIM ** -0.5
LOG2_E = float(np.log2(np.e))
Q_SCALE = ATTN_SCALE * LOG2_E
UNSELECTED = -(2.0 ** 100)

LANES = 128
BF16 = jnp.bfloat16
F32 = jnp.float32

ROW_TILE = 1024
OUT_ROW_TILE = 512
FFN_ROW_TILE = 1024
PROJ_COL_TILE = 1024
FF_TILE = 512
NSA_Q_TILE = 256
NSA_K_TILE = 512
SB_TILE = 256
SB_HEADS = 8
MOD_COL_TILE = 1024
VMEM_LIMIT = 56 * 1024 * 1024
FFN_VMEM_LIMIT = 62 * 1024 * 1024

_NT = (((1,), (1,)), ((), ()))


def _params(n_axes, vmem_limit=VMEM_LIMIT):
    return pltpu.CompilerParams(
        dimension_semantics=("arbitrary",) * n_axes, vmem_limit_bytes=vmem_limit)


def _dot(a, b):
    return jnp.dot(a, b, preferred_element_type=F32)


def _dot_nt(a, b):
    return lax.dot_general(a, b, _NT, preferred_element_type=F32)


def _split_bf16(x):
    hi = x.astype(BF16)
    lo = (x - hi.astype(F32)).astype(BF16)
    return hi, lo


def _rms(x, gamma):
    ms = jnp.mean(x * x, axis=-1, keepdims=True)
    return x * lax.rsqrt(ms + EPS) * gamma


def _mod_kernel(c_ref, w_ref, b_ref, o_ref):
    c = c_ref[...]
    ca = c * jax.nn.sigmoid(c)
    c_hi, c_lo = _split_bf16(ca)
    w_hi, w_lo = _split_bf16(w_ref[...])
    acc = _dot(c_hi, w_hi) + _dot(c_lo, w_hi) + _dot(c_hi, w_lo)
    o_ref[...] = acc + b_ref[...]


def _modulation(c_pad, w, b):
    n_layers, d, n = w.shape
    tn = MOD_COL_TILE
    assert n % tn == 0, (n, tn)
    return pl.pallas_call(
        _mod_kernel,
        grid=(n_layers, n // tn),
        in_specs=[
            pl.BlockSpec((8, d), lambda l, j: (0, 0)),
            pl.BlockSpec((None, d, tn), lambda l, j: (l, 0, j)),
            pl.BlockSpec((None, 1, tn), lambda l, j: (l, 0, j)),
        ],
        out_specs=pl.BlockSpec((None, 8, tn), lambda l, j: (l, 0, j)),
        out_shape=jax.ShapeDtypeStruct((n_layers, 8, n), F32),
        compiler_params=_params(2),
        name="modulation",
    )(c_pad, w, b)


def _normmod(x_ref, g_ref, sh_ref, sc_ref):
    y = _rms(x_ref[...], g_ref[...])
    return y * (1.0 + sc_ref[...]) + sh_ref[...]


def _proj_kernel(x_ref, g_ref, sh_ref, sc_ref, w_ref, *rest, mode, out_scale):
    if mode == "nsa":
        cos_ref, sin_ref, o_ref, chunk_ref, h_ref, relayout_ref = rest
    elif mode == "gate":
        b_ref, o_ref, h_ref = rest
    else:
        o_ref, h_ref = rest
    j = pl.program_id(1)

    @pl.when(j == 0)
    def _():
        h_ref[...] = _normmod(x_ref, g_ref, sh_ref, sc_ref).astype(BF16)

    acc = _dot(h_ref[...], w_ref[...])
    if mode == "plain":
        o_ref[...] = (acc * out_scale).astype(o_ref.dtype)
    elif mode == "gate":
        o_ref[...] = jax.nn.sigmoid(acc + b_ref[...])
    else:
        half = NSA_KV_HEADS
        is_q = j < 2
        scale = jnp.where(is_q, Q_SCALE, 1.0).astype(F32)
        cos_k = cos_ref[...] * scale
        sin_k = sin_ref[...] * scale
        cos_v = jnp.where(is_q, cos_k, 1.0)
        sin_v = jnp.where(is_q, sin_k, 0.0)

        def rotate(hh, cos, sin):
            xh = acc[:, hh * HEAD_DIM:(hh + 1) * HEAD_DIM]
            return xh * cos + pltpu.roll(xh, HEAD_DIM // 2, axis=1) * sin

        for hh in range(2 * half):
            rot = rotate(hh, cos_k, sin_k) if hh < half else rotate(hh, cos_v, sin_v)
            o_ref[:, hh * HEAD_DIM:(hh + 1) * HEAD_DIM] = rot.astype(o_ref.dtype)

        @pl.when(j == 2)
        def _():
            n_chunk_rows = chunk_ref.shape[2]
            for hh in range(2 * half):
                if hh < half:
                    relayout_ref[...] = rotate(hh, cos_ref[...], sin_ref[...])
                else:
                    relayout_ref[...] = acc[:, hh * HEAD_DIM:(hh + 1) * HEAD_DIM]
                for l in range(CMP_STRIDE):
                    rows_l = relayout_ref[pl.ds(l, n_chunk_rows, stride=CMP_STRIDE), :]
                    chunk_ref[hh // half, hh % half, :, l * HEAD_DIM:(l + 1) * HEAD_DIM] = (
                        rows_l.astype(chunk_ref.dtype))


def _project(x2d, seq, gamma, shift, scale, w, *, mode="plain", out_scale=1.0,
             out_dtype=BF16, extra=(), tn=PROJ_COL_TILE):
    t, d = x2d.shape
    n = w.shape[1]
    tm = min(ROW_TILE, seq)
    assert seq % tm == 0 and n % tn == 0, (seq, tm, n, tn)
    tiles_per_seq = seq // tm
    in_specs = [
        pl.BlockSpec((tm, d), lambda i, j: (i, 0)),
        pl.BlockSpec((1, d), lambda i, j: (0, 0)),
        pl.BlockSpec((None, 1, d), lambda i, j: (i // tiles_per_seq, 0, 0)),
        pl.BlockSpec((None, 1, d), lambda i, j: (i // tiles_per_seq, 0, 0)),
        pl.BlockSpec((d, tn), lambda i, j: (0, j)),
    ]
    out_specs = pl.BlockSpec((tm, tn), lambda i, j: (i, j))
    out_shape = jax.ShapeDtypeStruct((t, n), out_dtype)
    scratch_shapes = [pltpu.VMEM((tm, d), BF16)]
    if mode == "nsa":
        assert tn == 2 * NSA_KV_HEADS * HEAD_DIM and tm % CMP_STRIDE == 0, (tn, tm)
        in_specs += [pl.BlockSpec((tm, HEAD_DIM), lambda i, j: (i % tiles_per_seq, 0))] * 2
        chunk_rows = tm // CMP_STRIDE
        width = CMP_STRIDE * HEAD_DIM
        out_specs = [out_specs,
                     pl.BlockSpec((2, None, NSA_KV_HEADS, chunk_rows, width),
                                  lambda i, j: (0, i // tiles_per_seq, 0, i % tiles_per_seq, 0))]
        out_shape = [out_shape,
                     jax.ShapeDtypeStruct((2, t // seq, NSA_KV_HEADS, seq // CMP_STRIDE, width), BF16)]
        scratch_shapes += [pltpu.VMEM((tm, HEAD_DIM), F32)]
    elif mode == "gate":
        in_specs += [pl.BlockSpec((1, tn), lambda i, j: (0, j))]
    return pl.pallas_call(
        functools.partial(_proj_kernel, mode=mode, out_scale=out_scale),
        grid=(t // tm, n // tn),
        in_specs=in_specs,
        out_specs=out_specs,
        out_shape=out_shape,
        scratch_shapes=scratch_shapes,
        compiler_params=_params(2),
        name="proj_" + mode,
    )(x2d, gamma, shift, scale, w, *extra)


def _outproj_kernel(o_ref, w_ref, x_ref, gate_ref, g_ref, out_ref):
    y = _dot(o_ref[...], w_ref[...])
    out_ref[...] = x_ref[...] + gate_ref[...] * _rms(y, g_ref[...])


def _out_project(o2d, w, x2d, seq, gate, gamma):
    t, d = x2d.shape
    k = o2d.shape[1]
    tm = min(OUT_ROW_TILE, seq)
    assert seq % tm == 0, (seq, tm)
    tiles_per_seq = seq // tm
    return pl.pallas_call(
        _outproj_kernel,
        grid=(t // tm,),
        in_specs=[
            pl.BlockSpec((tm, k), lambda i: (i, 0)),
            pl.BlockSpec((k, d), lambda i: (0, 0)),
            pl.BlockSpec((tm, d), lambda i: (i, 0)),
            pl.BlockSpec((None, 1, d), lambda i: (i // tiles_per_seq, 0, 0)),
            pl.BlockSpec((1, d), lambda i: (0, 0)),
        ],
        out_specs=pl.BlockSpec((tm, d), lambda i: (i, 0)),
        out_shape=jax.ShapeDtypeStruct((t, d), F32),
        compiler_params=_params(1),
        name="out_proj",
    )(o2d, w, x2d, gate, gamma)


def _ffn_kernel(x_ref, g_in_ref, sh_ref, sc_ref, wg_ref, wu_ref, wo_ref, gate_ref, g_out_ref,
                out_ref, h_ref):
    f = pl.program_id(1)

    @pl.when(f == 0)
    def _():
        h_ref[...] = _normmod(x_ref, g_in_ref, sh_ref, sc_ref).astype(BF16)
        out_ref[...] = jnp.zeros_like(out_ref)

    h = h_ref[...]
    gate = _dot(h, wg_ref[...])
    up = _dot(h, wu_ref[...])
    act = (gate * jax.nn.sigmoid(gate) * up).astype(BF16)
    out_ref[...] += _dot(act, wo_ref[...])

    @pl.when(f == pl.num_programs(1) - 1)
    def _():
        out_ref[...] = x_ref[...] + gate_ref[...] * _rms(out_ref[...], g_out_ref[...])


def _ffn(x2d, seq, g_in, shift, scale, w_in, w_out, layer, gate, g_out):
    t, d = x2d.shape
    d_ff = w_out.shape[1]
    tm = min(FFN_ROW_TILE, seq)
    tf = FF_TILE
    assert d_ff % tf == 0 and seq % tm == 0, (d_ff, tf, seq, tm)
    n_f = d_ff // tf
    tiles_per_seq = seq // tm
    batch_vec = pl.BlockSpec((None, 1, d), lambda i, f: (i // tiles_per_seq, 0, 0))
    vec = pl.BlockSpec((1, d), lambda i, f: (0, 0))
    return pl.pallas_call(
        _ffn_kernel,
        grid=(t // tm, n_f),
        in_specs=[
            pl.BlockSpec((tm, d), lambda i, f: (i, 0)),
            vec, batch_vec, batch_vec,
            pl.BlockSpec((None, d, tf), lambda i, f: (layer, 0, f)),
            pl.BlockSpec((None, d, tf), lambda i, f: (layer, 0, n_f + f)),
            pl.BlockSpec((None, tf, d), lambda i, f: (layer, f, 0)),
            batch_vec, vec,
        ],
        out_specs=pl.BlockSpec((tm, d), lambda i, f: (i, 0)),
        out_shape=jax.ShapeDtypeStruct((t, d), F32),
        scratch_shapes=[pltpu.VMEM((tm, d), BF16)],
        compiler_params=_params(2, FFN_VMEM_LIMIT),
        name="ffn",
    )(x2d, g_in, shift, scale, w_in, w_in, w_out, gate, g_out)


def _gelu_tanh(x):
    return 0.5 * x * (1.0 + jnp.tanh(np.sqrt(2.0 / np.pi) * (x + 0.044715 * (x * x * x))))


def _compress_kernel(c_ref, w1_ref, w2_ref, pe_ref, o_ref):
    half = CMP_STRIDE * HEAD_DIM
    c = c_ref[...]
    n_chunks = c.shape[0]
    first = _dot(c, w1_ref[0:half, :])
    second = _dot(c, w1_ref[half:2 * half, :])
    pe = jnp.broadcast_to(pe_ref[...], (8, 2 * half)).astype(BF16)
    bias = _dot(pe, w1_ref[...])[0:1, :]
    hidden = first + pltpu.roll(second, n_chunks - 1, axis=0) + bias
    out = _dot(_gelu_tanh(hidden).astype(BF16), w2_ref[...])
    row = lax.broadcasted_iota(jnp.int32, out.shape, 0)
    o_ref[...] = jnp.where(row < n_chunks - 1, out, 0.0).astype(o_ref.dtype)


def _compress(chunks, w1, w2, pe):
    _, b, g, n_chunks, width = chunks.shape
    return pl.pallas_call(
        _compress_kernel,
        grid=(2, b, g),
        in_specs=[
            pl.BlockSpec((None, None, None, n_chunks, width), lambda w, bi, gi: (w, bi, gi, 0, 0)),
            pl.BlockSpec((None, 2 * width, CMP_HIDDEN), lambda w, bi, gi: (w, 0, 0)),
            pl.BlockSpec((None, CMP_HIDDEN, HEAD_DIM), lambda w, bi, gi: (w, 0, 0)),
            pl.BlockSpec((None, 1, 2 * width), lambda w, bi, gi: (w, 0, 0)),
        ],
        out_specs=pl.BlockSpec((None, None, None, n_chunks, HEAD_DIM),
                               lambda w, bi, gi: (w, bi, gi, 0, 0)),
        out_shape=jax.ShapeDtypeStruct((2, b, g, n_chunks, HEAD_DIM), BF16),
        compiler_params=_params(3),
        name="compress",
    )(chunks, w1, w2, pe)


def _nsa_kernel(q_ref, kc_ref, vc_ref, ks_ref, vs_ref, kw_ref, vw_ref, gate_ref, o_ref,
                kaug_ref, qaug_ref, m_ref, l_ref, acc_ref, side_ref, sa_ref, sb_ref):
    tq = q_ref.shape[0]
    tk = sa_ref.shape[1]
    r_heads = HEADS_PER_GROUP
    rows = r_heads * tq
    seq = ks_ref.shape[0]
    n_cmp = kc_ref.shape[0]
    qi = pl.program_id(2)
    t0 = qi * tq

    @pl.when(qi == 0)
    def _():
        kaug_ref[:, 0:HEAD_DIM] = ks_ref[...]
        row = lax.broadcasted_iota(jnp.int32, (seq, LANES), 0)
        lane = lax.broadcasted_iota(jnp.int32, (seq, LANES), 1)
        kaug_ref[:, HEAD_DIM:2 * HEAD_DIM] = jnp.where(
            (row // SEL_BLOCK) == lane, 1.0, 0.0).astype(BF16)

    for r in range(r_heads):
        qaug_ref[r * tq:(r + 1) * tq, 0:HEAD_DIM] = q_ref[:, r * HEAD_DIM:(r + 1) * HEAD_DIM]
    q_all = qaug_ref[:, 0:HEAD_DIM]
    t_row = t0 + (lax.broadcasted_iota(jnp.int32, (rows, 1), 0) & (tq - 1))
    gates = gate_ref[...]

    def gate_col(branch):
        return jnp.concatenate([gates[:, 3 * r + branch:3 * r + branch + 1] for r in range(r_heads)], axis=0)

    span = WINDOW + tq
    w_start = pl.multiple_of(jnp.maximum(t0 - WINDOW, 0), tq)
    s_c = _dot_nt(q_all, kc_ref[...])
    s_w = _dot_nt(q_all, kw_ref[pl.ds(w_start, span), :])

    n_idx = lax.broadcasted_iota(jnp.int32, (rows, n_cmp), 1)
    valid = (n_idx * CMP_STRIDE + (CMP_BLOCK - 1)) <= t_row
    any_valid = jnp.where(t_row >= CMP_BLOCK - 1, 1.0, 0.0).astype(F32)
    s_c = jnp.where(valid, s_c, NEG)
    e_c = jnp.exp2(s_c - jnp.max(s_c, axis=-1, keepdims=True))
    p_c = e_c / jnp.sum(e_c, axis=-1, keepdims=True) * any_valid
    o_cmp = _dot(p_c.astype(BF16), vc_ref[...])
    p_sum = p_c[0:tq]
    for r in range(1, r_heads):
        p_sum = p_sum + p_c[r * tq:(r + 1) * tq]

    kpos = w_start + lax.broadcasted_iota(jnp.int32, (rows, span), 1)
    in_window = (kpos <= t_row) & (kpos > t_row - WINDOW)
    s_w = jnp.where(in_window, s_w, NEG)
    e_w = jnp.exp2(s_w - jnp.max(s_w, axis=-1, keepdims=True))
    o_win = _dot(e_w.astype(BF16), vw_ref[pl.ds(w_start, span), :]) / jnp.sum(e_w, axis=-1, keepdims=True)
    side_ref[...] = gate_col(0) * o_cmp + gate_col(2) * o_win

    n_blocks = seq // SEL_BLOCK
    n_sel = min(SEL_TOPK, n_blocks)
    m_o = lax.broadcasted_iota(jnp.int32, (LANES, n_cmp), 0)
    n_o = lax.broadcasted_iota(jnp.int32, (LANES, n_cmp), 1)
    overlap_t = jnp.where(n_o * CMP_STRIDE < (m_o + 1) * SEL_BLOCK,
                          jnp.where(n_o * CMP_STRIDE + CMP_BLOCK > m_o * SEL_BLOCK, 1.0, 0.0),
                          0.0).astype(BF16)
    p_hi, p_lo = _split_bf16(p_sum)
    imp = (_dot_nt(overlap_t, p_hi) + _dot_nt(overlap_t, p_lo))[0:n_blocks]
    blk = lax.broadcasted_iota(jnp.int32, (n_blocks, tq), 0)
    t_s = t0 + lax.broadcasted_iota(jnp.int32, (n_blocks, tq), 1)
    cur = t_s // SEL_BLOCK
    forced = (blk == 0) | (blk == cur) | (blk == cur - 1)
    avail = blk * SEL_BLOCK <= t_s
    imp = jnp.where(forced, FORCE, jnp.where(avail, imp, -FORCE))
    sub = 8
    imp_v = [imp[v * sub:(v + 1) * sub] for v in range(n_blocks // sub)]
    blk_v = lax.broadcasted_iota(jnp.int32, (sub, tq), 0)
    rank_v = [jnp.zeros((sub, tq), F32) for _ in imp_v]
    for mp in range(n_blocks):
        other = imp[mp:mp + 1]
        for v in range(len(imp_v)):
            wins_ties = jnp.where(other >= imp_v[v], 1.0, 0.0)
            loses_ties = jnp.where(other > imp_v[v], 1.0, 0.0)
            if v * sub > mp:
                ahead = wins_ties
            elif (v + 1) * sub - 1 <= mp:
                ahead = loses_ties
            else:
                ahead = jnp.where(blk_v + v * sub > mp, wins_ties, loses_ties)
            rank_v[v] = rank_v[v] + ahead
    rank = jnp.concatenate(rank_v, axis=0)
    mask_t = jnp.where(rank < n_sel, 0.0, UNSELECTED)
    mask_t = jnp.concatenate([mask_t, jnp.zeros((LANES - n_blocks, tq), F32)], axis=0)
    mask_feat = mask_t.T.astype(BF16)
    for r in range(r_heads):
        qaug_ref[r * tq:(r + 1) * tq, HEAD_DIM:2 * HEAD_DIM] = mask_feat

    m_ref[...] = jnp.full_like(m_ref, NEG)
    l_ref[...] = jnp.zeros_like(l_ref)
    acc_ref[...] = jnp.zeros_like(acc_ref)
    def scores(j, dst_ref):
        start = pl.multiple_of(j * tk, tk)
        dst_ref[...] = _dot_nt(qaug_ref[...], kaug_ref[pl.ds(start, tk), :])

    def consume(src_ref, j, causal):
        start = pl.multiple_of(j * tk, tk)
        s = src_ref[...]
        if causal:
            kp = start + lax.broadcasted_iota(jnp.int32, (rows, tk), 1)
            s = jnp.where(kp <= t_row, s, NEG)
        m_old = m_ref[...]
        m_new = jnp.maximum(m_old, jnp.max(s, axis=-1, keepdims=True))
        alpha = jnp.exp2(m_old - m_new)
        p = jnp.exp2(s - m_new[:, 0:1])
        l_ref[...] = alpha * l_ref[...] + jnp.sum(p, axis=-1, keepdims=True)
        acc_ref[...] = alpha * acc_ref[...] + _dot(p.astype(BF16), vs_ref[pl.ds(start, tk), :])
        m_ref[...] = m_new

    n_full = t0 // tk
    n_pairs = n_full // 2
    buf_a = sa_ref
    buf_b = sb_ref
    scores(0, buf_a)

    def pair_body(i, carry):
        j = 2 * i
        scores(j + 1, buf_b)
        consume(buf_a, j, False)
        scores(j + 2, buf_a)
        consume(buf_b, j + 1, False)
        return carry

    lax.fori_loop(0, n_pairs, pair_body, 0)
    j_next = 2 * n_pairs

    @pl.when(j_next < n_full)
    def _():
        scores(j_next + 1, buf_b)
        consume(buf_a, j_next, False)
        consume(buf_b, j_next + 1, True)

    @pl.when(j_next == n_full)
    def _():
        consume(buf_a, j_next, True)

    out = side_ref[...] + gate_col(1) * (acc_ref[...] / l_ref[...])
    for r in range(r_heads):
        o_ref[:, r * HEAD_DIM:(r + 1) * HEAD_DIM] = out[r * tq:(r + 1) * tq].astype(o_ref.dtype)


def _nsa_attention(proj, comp, gates, batch, seq, tq=NSA_Q_TILE, tk=NSA_K_TILE):
    t = proj.shape[0]
    assert seq % tk == 0 and tk % tq == 0 and seq % SEL_BLOCK == 0, (seq, tq, tk)
    n_q = seq // tq
    rows = HEADS_PER_GROUP * tq
    n_chunks = comp.shape[3]
    kv_cols = NSA_KV_HEADS
    q_cols = N_HEADS

    def kv_spec(which):
        base = q_cols + which * kv_cols
        return pl.BlockSpec((seq, HEAD_DIM), lambda b, g, qi: (b, base + g))

    def cmp_spec(which):
        return pl.BlockSpec((None, None, None, n_chunks, HEAD_DIM),
                            lambda b, g, qi: (which, b, g, 0, 0))

    return pl.pallas_call(
        _nsa_kernel,
        grid=(batch, NSA_KV_HEADS, n_q),
        in_specs=[
            pl.BlockSpec((tq, HEADS_PER_GROUP * HEAD_DIM), lambda b, g, qi: (b * n_q + qi, g)),
            cmp_spec(0), cmp_spec(1),
            kv_spec(2), kv_spec(3), kv_spec(4), kv_spec(5),
            pl.BlockSpec((tq, LANES), lambda b, g, qi: (b * n_q + qi, g)),
        ],
        out_specs=pl.BlockSpec((tq, HEADS_PER_GROUP * HEAD_DIM), lambda b, g, qi: (b * n_q + qi, g)),
        out_shape=jax.ShapeDtypeStruct((t, N_HEADS * HEAD_DIM), BF16),
        scratch_shapes=[
            pltpu.VMEM((seq, 2 * HEAD_DIM), BF16),
            pltpu.VMEM((rows, 2 * HEAD_DIM), BF16),
            pltpu.VMEM((rows, LANES), F32),
            pltpu.VMEM((rows, LANES), F32),
            pltpu.VMEM((rows, HEAD_DIM), F32),
            pltpu.VMEM((rows, HEAD_DIM), F32),
            pltpu.VMEM((rows, tk), F32),
            pltpu.VMEM((rows, tk), F32),
        ],
        compiler_params=_params(3),
        name="nsa_attention",
    )(proj, comp, comp, proj, proj, proj, proj, gates)


def _sb_kernel(q_ref, k_ref, v_ref, o_ref, carry_ref, acc_ref, za_ref, zb_ref):
    tk = SB_TILE
    qi = pl.program_id(2)
    row = lax.broadcasted_iota(jnp.int32, (tk, tk), 0)
    col = lax.broadcasted_iota(jnp.int32, (tk, tk), 1)
    causal = col < row
    not_before = jnp.where(row >= col, 1.0, 0.0).astype(BF16)
    not_before2 = jnp.concatenate([not_before, not_before], axis=0)
    carry_ref[...] = jnp.zeros_like(carry_ref)
    acc_ref[...] = jnp.zeros_like(acc_ref)
    sign_bit = jnp.uint32(0x80000000)

    heads = range(carry_ref.shape[0])
    hs = [slice(h * HEAD_DIM, (h + 1) * HEAD_DIM) for h in heads]

    def scores(j, dst_ref):
        start = pl.multiple_of(j * tk, tk)
        for h in heads:
            dst_ref[h] = _dot_nt(q_ref[:, hs[h]], k_ref[pl.ds(start, tk), hs[h]])

    def consume(src_ref, j, diagonal):
        start = pl.multiple_of(j * tk, tk)
        z = [src_ref[h] for h in heads]
        neg_abs = [pltpu.bitcast(pltpu.bitcast(z[h], jnp.uint32) | sign_bit, F32) for h in heads]
        sp = [jnp.maximum(z[h], 0.0) + jnp.log(1.0 + jnp.exp2(neg_abs[h])) * LOG2_E for h in heads]
        if diagonal:
            sp = [jnp.where(causal, sp[h], 0.0) for h in heads]
        split = [jnp.concatenate(_split_bf16(sp[h]), axis=1) for h in heads]
        suffix = [_dot(split[h], not_before2) for h in heads]
        carry = [carry_ref[h] for h in heads]
        a = [jnp.exp2(z[h] - suffix[h] - jnp.concatenate([carry[h]] * (tk // LANES), axis=1))
             for h in heads]
        if diagonal:
            a = [jnp.where(causal, a[h], 0.0) for h in heads]
        for h in heads:
            acc_ref[h] += _dot(a[h].astype(BF16), v_ref[pl.ds(start, tk), hs[h]])
            carry_ref[h] = carry[h] + suffix[h][:, 0:1]

    scores(qi, za_ref)
    scores(jnp.maximum(qi - 1, 0), zb_ref)
    consume(za_ref, qi, True)
    n_pairs = jnp.maximum(qi - 1, 0) // 2

    def pair_body(i, c):
        j = qi - 1 - 2 * i
        scores(j - 1, za_ref)
        consume(zb_ref, j, False)
        scores(j - 2, zb_ref)
        consume(za_ref, j - 1, False)
        return c

    lax.fori_loop(0, n_pairs, pair_body, 0)
    j_next = qi - 1 - 2 * n_pairs

    @pl.when(j_next == 1)
    def _():
        scores(0, za_ref)
        consume(zb_ref, 1, False)
        consume(za_ref, 0, False)

    @pl.when(j_next == 0)
    def _():
        consume(zb_ref, 0, False)

    for h in heads:
        o_ref[:, h * HEAD_DIM:(h + 1) * HEAD_DIM] = acc_ref[h].astype(o_ref.dtype)


def _sb_attention(q, kv, batch, seq, n_heads=SB_HEADS):
    t = q.shape[0]
    tq = SB_TILE
    assert seq % tq == 0 and N_HEADS % n_heads == 0, (seq, tq, n_heads)
    n_q = seq // tq
    width = n_heads * HEAD_DIM
    n_hb = N_HEADS // n_heads
    return pl.pallas_call(
        _sb_kernel,
        grid=(batch, n_hb, n_q),
        in_specs=[
            pl.BlockSpec((tq, width), lambda b, h, qi: (b * n_q + qi, h)),
            pl.BlockSpec((seq, width), lambda b, h, qi: (b, h)),
            pl.BlockSpec((seq, width), lambda b, h, qi: (b, n_hb + h)),
        ],
        out_specs=pl.BlockSpec((tq, width), lambda b, h, qi: (b * n_q + qi, h)),
        out_shape=jax.ShapeDtypeStruct((t, N_HEADS * HEAD_DIM), BF16),
        scratch_shapes=[pltpu.VMEM((n_heads, tq, LANES), F32),
                        pltpu.VMEM((n_heads, tq, HEAD_DIM), F32),
                        pltpu.VMEM((n_heads, tq, SB_TILE), F32),
                        pltpu.VMEM((n_heads, tq, SB_TILE), F32)],
        compiler_params=_params(3),
        name="sb_attention",
    )(q, kv, kv)


def _rope_tables(seq):
    half = HEAD_DIM // 2
    inv = ROPE_THETA ** (-jnp.arange(half, dtype=F32) / half)
    ang = jnp.arange(seq, dtype=F32)[:, None] * inv[None, :]
    cos, sin = jnp.cos(ang), jnp.sin(ang)
    return jnp.concatenate([cos, cos], axis=-1), jnp.concatenate([-sin, sin], axis=-1)


def kernel(x, c, mod_w, mod_b, norm_g, ffn_w_in, ffn_w_out, a_w_in, a_gate_b, a_cmp_pe, a_cmp_w1,
           a_cmp_w2, a_w_out, b_w_q, b_w_out, kv_norm_g, kv_mod_w, kv_mod_b, kv_w):
    batch, seq, d = x.shape
    depth = mod_w.shape[0]
    n_a = a_w_in.shape[0]
    t = batch * seq
    hd = N_HEADS * HEAD_DIM
    kvd = NSA_KV_HEADS * HEAD_DIM
    x2d = x.reshape(t, d)

    c_pad = jnp.pad(c, ((0, 8 - batch), (0, 0)))
    mod = _modulation(c_pad, mod_w, mod_b[:, None, :])[:, :batch]
    mod = mod.reshape(depth, batch, 6, 1, d)
    kv_mod = _modulation(c_pad, kv_mod_w[None], kv_mod_b[None, None, :])[0, :batch]
    kv_mod = kv_mod.reshape(batch, 2, 1, d)
    cos, sin = _rope_tables(seq)
    ffn_w_in_bf16 = ffn_w_in.astype(BF16)
    ffn_w_out_bf16 = ffn_w_out.astype(BF16)

    kv = None
    for layer in range(depth):
        sh1, sc1, g1, sh2, sc2, g2 = (mod[layer, :, i] for i in range(6))
        gam = norm_g[layer][:, None, :]
        if layer < n_a:
            w_in = a_w_in[layer]
            n_main = hd + 6 * kvd
            proj, chunks = _project(x2d, seq, gam[0], sh1, sc1, w_in[:, :n_main].astype(BF16),
                                    mode="nsa", extra=(cos, sin), tn=2 * kvd)
            n_gate = 3 * HEADS_PER_GROUP
            w_gate = w_in[:, n_main:].reshape(d, NSA_KV_HEADS, n_gate)
            w_gate = jnp.pad(w_gate, ((0, 0), (0, 0), (0, LANES - n_gate))).reshape(d, NSA_KV_HEADS * LANES)
            b_gate = jnp.pad(a_gate_b[layer].reshape(NSA_KV_HEADS, n_gate),
                             ((0, 0), (0, LANES - n_gate))).reshape(1, NSA_KV_HEADS * LANES)
            gates = _project(x2d, seq, gam[0], sh1, sc1, w_gate.astype(BF16), mode="gate",
                             out_dtype=F32, extra=(b_gate,), tn=NSA_KV_HEADS * LANES)
            comp = _compress(chunks, a_cmp_w1[layer].astype(BF16), a_cmp_w2[layer].astype(BF16),
                             a_cmp_pe[layer].reshape(2, 1, CMP_BLOCK * HEAD_DIM))
            o = _nsa_attention(proj, comp, gates, batch, seq)
            x2d = _out_project(o, a_w_out[layer].astype(BF16), x2d, seq, g1, gam[1])
        else:
            jb = layer - n_a
            if kv is None:
                kv = _project(x2d, seq, kv_norm_g[None, :], kv_mod[:, 0], kv_mod[:, 1],
                              kv_w.astype(BF16))
            q = _project(x2d, seq, gam[0], sh1, sc1, b_w_q[jb].astype(BF16), out_scale=Q_SCALE)
            o = _sb_attention(q, kv, batch, seq)
            x2d = _out_project(o, b_w_out[jb].astype(BF16), x2d, seq, g1, gam[1])
        x2d = _ffn(x2d, seq, gam[2], sh2, sc2, ffn_w_in_bf16, ffn_w_out_bf16, layer, g2, gam[3])
    return x2d.reshape(batch, seq, d)
```

```python
import functools

import jax
import jax.numpy as jnp
import numpy as np
from jax import lax
from jax.experimental import pallas as pl
from jax.experimental.pallas import tpu as pltpu

N_HEADS = 16
HEAD_DIM = 128
NSA_KV_HEADS = 4
HEADS_PER_GROUP = N_HEADS // NSA_KV_HEADS
CMP_BLOCK = 32
CMP_STRIDE = 16
CMP_HIDDEN = 256
SEL_BLOCK = 64
SEL_TOPK = 16
WINDOW = 512
D_FF = 5632
ROPE_THETA = 10000.0
EPS = 1e-6
NEG = -1e30
FORCE = 1e9
ATTN_SCALE = HEAD_DIM ** -0.5
LOG2_E = float(np.log2(np.e))
Q_SCALE = ATTN_SCALE * LOG2_E
UNSELECTED = -(2.0 ** 100)

LANES = 128
BF16 = jnp.bfloat16
F32 = jnp.float32

ROW_TILE = 1024
OUT_ROW_TILE = 512
FFN_ROW_TILE = 1024
PROJ_COL_TILE = 1024
FF_TILE = 512
NSA_Q_TILE = 256
NSA_K_TILE = 512
SB_TILE = 256
SB_HEADS = 8
MOD_COL_TILE = 1024
VMEM_LIMIT = 56 * 1024 * 1024
FFN_VMEM_LIMIT = 62 * 1024 * 1024

_NT = (((1,), (1,)), ((), ()))


def _params(n_axes, vmem_limit=VMEM_LIMIT):
    return pltpu.CompilerParams(
        dimension_semantics=("arbitrary",) * n_axes, vmem_limit_bytes=vmem_limit)


def _dot(a, b):
    return jnp.dot(a, b, preferred_element_type=F32)


def _dot_nt(a, b):
    return lax.dot_general(a, b, _NT, preferred_element_type=F32)


def _split_bf16(x):
    hi = x.astype(BF16)
    lo = (x - hi.astype(F32)).astype(BF16)
    return hi, lo


NORM_ROWS = 16
NORM_UNROLL = 8


def _normmod_store(x_ref, g_ref, sh_ref, sc_ref, h_ref):
    gain = g_ref[...] * (1.0 + sc_ref[...])
    shift = sh_ref[...]

    def body(c, carry):
        r0 = pl.multiple_of(c * NORM_ROWS, NORM_ROWS)
        x = x_ref[pl.ds(r0, NORM_ROWS), :]
        ms = jnp.mean(x * x, axis=-1, keepdims=True)
        h_ref[pl.ds(r0, NORM_ROWS), :] = (x * lax.rsqrt(ms + EPS) * gain + shift).astype(h_ref.dtype)
        return carry

    lax.fori_loop(0, x_ref.shape[0] // NORM_ROWS, body, 0, unroll=NORM_UNROLL)


def _residual_norm_inplace(out_ref, x_ref, gate_ref, g_ref):
    gain = gate_ref[...] * g_ref[...]
    rows = NORM_ROWS // 2
    for r0 in range(0, out_ref.shape[0], rows):
        y = out_ref[r0:r0 + rows, :]
        ms = jnp.mean(y * y, axis=-1, keepdims=True)
        out_ref[r0:r0 + rows, :] = x_ref[r0:r0 + rows, :] + y * lax.rsqrt(ms + EPS) * gain


def _mod_kernel(c_ref, w_ref, b_ref, o_ref):
    c = c_ref[...]
    ca = c * jax.nn.sigmoid(c)
    c_hi, c_lo = _split_bf16(ca)
    w_hi, w_lo = _split_bf16(w_ref[...])
    acc = _dot(c_hi, w_hi) + _dot(c_lo, w_hi) + _dot(c_hi, w_lo)
    o_ref[...] = acc + b_ref[...]


def _modulation(c_pad, w, b):
    n_layers, d, n = w.shape
    tn = MOD_COL_TILE
    assert n % tn == 0, (n, tn)
    return pl.pallas_call(
        _mod_kernel,
        grid=(n_layers, n // tn),
        in_specs=[
            pl.BlockSpec((8, d), lambda l, j: (0, 0)),
            pl.BlockSpec((None, d, tn), lambda l, j: (l, 0, j)),
            pl.BlockSpec((None, 1, tn), lambda l, j: (l, 0, j)),
        ],
        out_specs=pl.BlockSpec((None, 8, tn), lambda l, j: (l, 0, j)),
        out_shape=jax.ShapeDtypeStruct((n_layers, 8, n), F32),
        compiler_params=_params(2),
        name="modulation",
    )(c_pad, w, b)


def _proj_kernel(x_ref, g_ref, sh_ref, sc_ref, w_ref, *rest, mode, out_scale):
    if mode == "nsa":
        cos_ref, sin_ref, o_ref, chunk_ref, h_ref, relayout_ref = rest
    elif mode == "gate":
        b_ref, o_ref, h_ref = rest
    else:
        o_ref, h_ref = rest
    j = pl.program_id(1)

    @pl.when(j == 0)
    def _():
        _normmod_store(x_ref, g_ref, sh_ref, sc_ref, h_ref)

    acc = _dot(h_ref[...], w_ref[...])
    if mode == "plain":
        o_ref[...] = (acc * out_scale).astype(o_ref.dtype)
    elif mode == "gate":
        o_ref[...] = jax.nn.sigmoid(acc + b_ref[...])
    else:
        half = NSA_KV_HEADS
        is_q = j < 2
        scale = jnp.where(is_q, Q_SCALE, 1.0).astype(F32)
        cos_k = cos_ref[...] * scale
        sin_k = sin_ref[...] * scale
        cos_v = jnp.where(is_q, cos_k, 1.0)
        sin_v = jnp.where(is_q, sin_k, 0.0)

        def rotate(hh, cos, sin):
            xh = acc[:, hh * HEAD_DIM:(hh + 1) * HEAD_DIM]
            return xh * cos + pltpu.roll(xh, HEAD_DIM // 2, axis=1) * sin

        for hh in range(2 * half):
            rot = rotate(hh, cos_k, sin_k) if hh < half else rotate(hh, cos_v, sin_v)
            o_ref[:, hh * HEAD_DIM:(hh + 1) * HEAD_DIM] = rot.astype(o_ref.dtype)

        @pl.when(j == 2)
        def _():
            n_chunk_rows = chunk_ref.shape[2]
            for hh in range(2 * half):
                if hh < half:
                    relayout_ref[...] = rotate(hh, cos_ref[...], sin_ref[...])
                else:
                    relayout_ref[...] = acc[:, hh * HEAD_DIM:(hh + 1) * HEAD_DIM]
                for l in range(CMP_STRIDE):
                    rows_l = relayout_ref[pl.ds(l, n_chunk_rows, stride=CMP_STRIDE), :]
                    chunk_ref[hh // half, hh % half, :, l * HEAD_DIM:(l + 1) * HEAD_DIM] = (
                        rows_l.astype(chunk_ref.dtype))


def _project(x2d, seq, gamma, shift, scale, w, *, mode="plain", out_scale=1.0,
             out_dtype=BF16, extra=(), tn=PROJ_COL_TILE):
    t, d = x2d.shape
    n = w.shape[1]
    tm = min(ROW_TILE, seq)
    assert seq % tm == 0 and n % tn == 0, (seq, tm, n, tn)
    tiles_per_seq = seq // tm
    in_specs = [
        pl.BlockSpec((tm, d), lambda i, j: (i, 0)),
        pl.BlockSpec((1, d), lambda i, j: (0, 0)),
        pl.BlockSpec((None, 1, d), lambda i, j: (i // tiles_per_seq, 0, 0)),
        pl.BlockSpec((None, 1, d), lambda i, j: (i // tiles_per_seq, 0, 0)),
        pl.BlockSpec((d, tn), lambda i, j: (0, j)),
    ]
    out_specs = pl.BlockSpec((tm, tn), lambda i, j: (i, j))
    out_shape = jax.ShapeDtypeStruct((t, n), out_dtype)
    scratch_shapes = [pltpu.VMEM((tm, d), BF16)]
    if mode == "nsa":
        assert tn == 2 * NSA_KV_HEADS * HEAD_DIM and tm % CMP_STRIDE == 0, (tn, tm)
        in_specs += [pl.BlockSpec((tm, HEAD_DIM), lambda i, j: (i % tiles_per_seq, 0))] * 2
        chunk_rows = tm // CMP_STRIDE
        width = CMP_STRIDE * HEAD_DIM
        out_specs = [out_specs,
                     pl.BlockSpec((2, None, NSA_KV_HEADS, chunk_rows, width),
                                  lambda i, j: (0, i // tiles_per_seq, 0, i % tiles_per_seq, 0))]
        out_shape = [out_shape,
                     jax.ShapeDtypeStruct((2, t // seq, NSA_KV_HEADS, seq // CMP_STRIDE, width), BF16)]
        scratch_shapes += [pltpu.VMEM((tm, HEAD_DIM), F32)]
    elif mode == "gate":
        in_specs += [pl.BlockSpec((1, tn), lambda i, j: (0, j))]
    return pl.pallas_call(
        functools.partial(_proj_kernel, mode=mode, out_scale=out_scale),
        grid=(t // tm, n // tn),
        in_specs=in_specs,
        out_specs=out_specs,
        out_shape=out_shape,
        scratch_shapes=scratch_shapes,
        compiler_params=_params(2),
        name="proj_" + mode,
    )(x2d, gamma, shift, scale, w, *extra)


def _outproj_kernel(o_ref, w_ref, x_ref, gate_ref, g_ref, out_ref):
    out_ref[...] = _dot(o_ref[...], w_ref[...])
    _residual_norm_inplace(out_ref, x_ref, gate_ref, g_ref)


def _out_project(o2d, w, x2d, seq, gate, gamma):
    t, d = x2d.shape
    k = o2d.shape[1]
    tm = min(OUT_ROW_TILE, seq)
    assert seq % tm == 0, (seq, tm)
    tiles_per_seq = seq // tm
    return pl.pallas_call(
        _outproj_kernel,
        grid=(t // tm,),
        in_specs=[
            pl.BlockSpec((tm, k), lambda i: (i, 0)),
            pl.BlockSpec((k, d), lambda i: (0, 0)),
            pl.BlockSpec((tm, d), lambda i: (i, 0)),
            pl.BlockSpec((None, 1, d), lambda i: (i // tiles_per_seq, 0, 0)),
            pl.BlockSpec((1, d), lambda i: (0, 0)),
        ],
        out_specs=pl.BlockSpec((tm, d), lambda i: (i, 0)),
        out_shape=jax.ShapeDtypeStruct((t, d), F32),
        compiler_params=_params(1),
        name="out_proj",
    )(o2d, w, x2d, gate, gamma)


def _ffn_kernel(x_ref, g_in_ref, sh_ref, sc_ref, wg_ref, wu_ref, wo_ref, gate_ref, g_out_ref,
                out_ref, h_ref):
    f = pl.program_id(1)

    @pl.when(f == 0)
    def _():
        _normmod_store(x_ref, g_in_ref, sh_ref, sc_ref, h_ref)
        out_ref[...] = jnp.zeros_like(out_ref)

    h = h_ref[...]
    gate = _dot(h, wg_ref[...])
    up = _dot(h, wu_ref[...])
    act = (gate * jax.nn.sigmoid(gate) * up).astype(BF16)
    out_ref[...] += _dot(act, wo_ref[...])

    @pl.when(f == pl.num_programs(1) - 1)
    def _():
        _residual_norm_inplace(out_ref, x_ref, gate_ref, g_out_ref)


def _ffn(x2d, seq, g_in, shift, scale, w_in, w_out, layer, gate, g_out):
    t, d = x2d.shape
    d_ff = w_out.shape[1]
    tm = min(FFN_ROW_TILE, seq)
    tf = FF_TILE
    assert d_ff % tf == 0 and seq % tm == 0, (d_ff, tf, seq, tm)
    n_f = d_ff // tf
    tiles_per_seq = seq // tm
    batch_vec = pl.BlockSpec((None, 1, d), lambda i, f: (i // tiles_per_seq, 0, 0))
    vec = pl.BlockSpec((1, d), lambda i, f: (0, 0))
    return pl.pallas_call(
        _ffn_kernel,
        grid=(t // tm, n_f),
        in_specs=[
            pl.BlockSpec((tm, d), lambda i, f: (i, 0)),
            vec, batch_vec, batch_vec,
            pl.BlockSpec((None, d, tf), lambda i, f: (layer, 0, f)),
            pl.BlockSpec((None, d, tf), lambda i, f: (layer, 0, n_f + f)),
            pl.BlockSpec((None, tf, d), lambda i, f: (layer, f, 0)),
            batch_vec, vec,
        ],
        out_specs=pl.BlockSpec((tm, d), lambda i, f: (i, 0)),
        out_shape=jax.ShapeDtypeStruct((t, d), F32),
        scratch_shapes=[pltpu.VMEM((tm, d), BF16)],
        compiler_params=_params(2, FFN_VMEM_LIMIT),
        name="ffn",
    )(x2d, g_in, shift, scale, w_in, w_in, w_out, gate, g_out)


def _gelu_tanh(x):
    return 0.5 * x * (1.0 + jnp.tanh(np.sqrt(2.0 / np.pi) * (x + 0.044715 * (x * x * x))))


def _compress_kernel(c_ref, w1_ref, w2_ref, pe_ref, o_ref):
    half = CMP_STRIDE * HEAD_DIM
    c = c_ref[...]
    n_chunks = c.shape[0]
    first = _dot(c, w1_ref[0:half, :])
    second = _dot(c, w1_ref[half:2 * half, :])
    pe = jnp.broadcast_to(pe_ref[...], (8, 2 * half)).astype(BF16)
    bias = _dot(pe, w1_ref[...])[0:1, :]
    hidden = first + pltpu.roll(second, n_chunks - 1, axis=0) + bias
    out = _dot(_gelu_tanh(hidden).astype(BF16), w2_ref[...])
    row = lax.broadcasted_iota(jnp.int32, out.shape, 0)
    o_ref[...] = jnp.where(row < n_chunks - 1, out, 0.0).astype(o_ref.dtype)


def _compress(chunks, w1, w2, pe):
    _, b, g, n_chunks, width = chunks.shape
    return pl.pallas_call(
        _compress_kernel,
        grid=(2, b, g),
        in_specs=[
            pl.BlockSpec((None, None, None, n_chunks, width), lambda w, bi, gi: (w, bi, gi, 0, 0)),
            pl.BlockSpec((None, 2 * width, CMP_HIDDEN), lambda w, bi, gi: (w, 0, 0)),
            pl.BlockSpec((None, CMP_HIDDEN, HEAD_DIM), lambda w, bi, gi: (w, 0, 0)),
            pl.BlockSpec((None, 1, 2 * width), lambda w, bi, gi: (w, 0, 0)),
        ],
        out_specs=pl.BlockSpec((None, None, None, n_chunks, HEAD_DIM),
                               lambda w, bi, gi: (w, bi, gi, 0, 0)),
        out_shape=jax.ShapeDtypeStruct((2, b, g, n_chunks, HEAD_DIM), BF16),
        compiler_params=_params(3),
        name="compress",
    )(chunks, w1, w2, pe)


def _nsa_kernel(q_ref, kc_ref, vc_ref, ks_ref, vs_ref, kw_ref, vw_ref, gate_ref, o_ref,
                kaug_ref, qaug_ref, m_ref, l_ref, acc_ref, side_ref, sa_ref, sb_ref):
    tq = q_ref.shape[0]
    tk = sa_ref.shape[1]
    r_heads = HEADS_PER_GROUP
    rows = r_heads * tq
    seq = ks_ref.shape[0]
    n_cmp = kc_ref.shape[0]
    qi = pl.program_id(2)
    t0 = qi * tq

    @pl.when(qi == 0)
    def _():
        kaug_ref[:, 0:HEAD_DIM] = ks_ref[...]
        row = lax.broadcasted_iota(jnp.int32, (seq, LANES), 0)
        lane = lax.broadcasted_iota(jnp.int32, (seq, LANES), 1)
        kaug_ref[:, HEAD_DIM:2 * HEAD_DIM] = jnp.where(
            (row // SEL_BLOCK) == lane, 1.0, 0.0).astype(BF16)

    for r in range(r_heads):
        qaug_ref[r * tq:(r + 1) * tq, 0:HEAD_DIM] = q_ref[:, r * HEAD_DIM:(r + 1) * HEAD_DIM]
    q_all = qaug_ref[:, 0:HEAD_DIM]
    t_row = t0 + (lax.broadcasted_iota(jnp.int32, (rows, 1), 0) & (tq - 1))
    gates = gate_ref[...]

    def gate_col(branch):
        return jnp.concatenate([gates[:, 3 * r + branch:3 * r + branch + 1] for r in range(r_heads)], axis=0)

    span = WINDOW + tq
    w_start = pl.multiple_of(jnp.maximum(t0 - WINDOW, 0), tq)
    s_c = _dot_nt(q_all, kc_ref[...])
    s_w = _dot_nt(q_all, kw_ref[pl.ds(w_start, span), :])

    n_idx = lax.broadcasted_iota(jnp.int32, (rows, n_cmp), 1)
    valid = (n_idx * CMP_STRIDE + (CMP_BLOCK - 1)) <= t_row
    any_valid = jnp.where(t_row >= CMP_BLOCK - 1, 1.0, 0.0).astype(F32)
    s_c = jnp.where(valid, s_c, NEG)
    e_c = jnp.exp2(s_c - jnp.max(s_c, axis=-1, keepdims=True))
    p_c = e_c / jnp.sum(e_c, axis=-1, keepdims=True) * any_valid
    o_cmp = _dot(p_c.astype(BF16), vc_ref[...])
    p_sum = p_c[0:tq]
    for r in range(1, r_heads):
        p_sum = p_sum + p_c[r * tq:(r + 1) * tq]

    kpos = w_start + lax.broadcasted_iota(jnp.int32, (rows, span), 1)
    in_window = (kpos <= t_row) & (kpos > t_row - WINDOW)
    s_w = jnp.where(in_window, s_w, NEG)
    e_w = jnp.exp2(s_w - jnp.max(s_w, axis=-1, keepdims=True))
    o_win = _dot(e_w.astype(BF16), vw_ref[pl.ds(w_start, span), :]) / jnp.sum(e_w, axis=-1, keepdims=True)
    side_ref[...] = gate_col(0) * o_cmp + gate_col(2) * o_win

    n_blocks = seq // SEL_BLOCK
    n_sel = min(SEL_TOPK, n_blocks)
    m_o = lax.broadcasted_iota(jnp.int32, (LANES, n_cmp), 0)
    n_o = lax.broadcasted_iota(jnp.int32, (LANES, n_cmp), 1)
    overlap_t = jnp.where(n_o * CMP_STRIDE < (m_o + 1) * SEL_BLOCK,
                          jnp.where(n_o * CMP_STRIDE + CMP_BLOCK > m_o * SEL_BLOCK, 1.0, 0.0),
                          0.0).astype(BF16)
    p_hi, p_lo = _split_bf16(p_sum)
    imp = (_dot_nt(overlap_t, p_hi) + _dot_nt(overlap_t, p_lo))[0:n_blocks]
    blk = lax.broadcasted_iota(jnp.int32, (n_blocks, tq), 0)
    t_s = t0 + lax.broadcasted_iota(jnp.int32, (n_blocks, tq), 1)
    cur = t_s // SEL_BLOCK
    forced = (blk == 0) | (blk == cur) | (blk == cur - 1)
    avail = blk * SEL_BLOCK <= t_s
    imp = jnp.where(forced, FORCE, jnp.where(avail, imp, -FORCE))
    sub = 8
    imp_v = [imp[v * sub:(v + 1) * sub] for v in range(n_blocks // sub)]
    blk_v = lax.broadcasted_iota(jnp.int32, (sub, tq), 0)
    rank_v = [jnp.zeros((sub, tq), F32) for _ in imp_v]
    for mp in range(n_blocks):
        other = imp[mp:mp + 1]
        for v in range(len(imp_v)):
            wins_ties = jnp.where(other >= imp_v[v], 1.0, 0.0)
            loses_ties = jnp.where(other > imp_v[v], 1.0, 0.0)
            if v * sub > mp:
                ahead = wins_ties
            elif (v + 1) * sub - 1 <= mp:
                ahead = loses_ties
            else:
                ahead = jnp.where(blk_v + v * sub > mp, wins_ties, loses_ties)
            rank_v[v] = rank_v[v] + ahead
    rank = jnp.concatenate(rank_v, axis=0)
    mask_t = jnp.where(rank < n_sel, 0.0, UNSELECTED)
    mask_t = jnp.concatenate([mask_t, jnp.zeros((LANES - n_blocks, tq), F32)], axis=0)
    mask_feat = mask_t.T.astype(BF16)
    for r in range(r_heads):
        qaug_ref[r * tq:(r + 1) * tq, HEAD_DIM:2 * HEAD_DIM] = mask_feat

    m_ref[...] = jnp.full_like(m_ref, NEG)
    l_ref[...] = jnp.zeros_like(l_ref)
    acc_ref[...] = jnp.zeros_like(acc_ref)
    def scores(j, dst_ref):
        start = pl.multiple_of(j * tk, tk)
        dst_ref[...] = _dot_nt(qaug_ref[...], kaug_ref[pl.ds(start, tk), :])

    def consume(src_ref, j, causal):
        start = pl.multiple_of(j * tk, tk)
        s = src_ref[...]
        if causal:
            kp = start + lax.broadcasted_iota(jnp.int32, (rows, tk), 1)
            s = jnp.where(kp <= t_row, s, NEG)
        m_old = m_ref[...]
        m_new = jnp.maximum(m_old, jnp.max(s, axis=-1, keepdims=True))
        alpha = jnp.exp2(m_old - m_new)
        p = jnp.exp2(s - m_new[:, 0:1])
        l_ref[...] = alpha * l_ref[...] + jnp.sum(p, axis=-1, keepdims=True)
        acc_ref[...] = alpha * acc_ref[...] + _dot(p.astype(BF16), vs_ref[pl.ds(start, tk), :])
        m_ref[...] = m_new

    n_full = t0 // tk
    n_pairs = n_full // 2
    buf_a = sa_ref
    buf_b = sb_ref
    scores(0, buf_a)

    def pair_body(i, carry):
        j = 2 * i
        scores(j + 1, buf_b)
        consume(buf_a, j, False)
        scores(j + 2, buf_a)
        consume(buf_b, j + 1, False)
        return carry

    lax.fori_loop(0, n_pairs, pair_body, 0)
    j_next = 2 * n_pairs

    @pl.when(j_next < n_full)
    def _():
        scores(j_next + 1, buf_b)
        consume(buf_a, j_next, False)
        consume(buf_b, j_next + 1, True)

    @pl.when(j_next == n_full)
    def _():
        consume(buf_a, j_next, True)

    out = side_ref[...] + gate_col(1) * (acc_ref[...] / l_ref[...])
    for r in range(r_heads):
        o_ref[:, r * HEAD_DIM:(r + 1) * HEAD_DIM] = out[r * tq:(r + 1) * tq].astype(o_ref.dtype)


def _nsa_attention(proj, comp, gates, batch, seq, tq=NSA_Q_TILE, tk=NSA_K_TILE):
    t = proj.shape[0]
    assert seq % tk == 0 and tk % tq == 0 and seq % SEL_BLOCK == 0, (seq, tq, tk)
    n_q = seq // tq
    rows = HEADS_PER_GROUP * tq
    n_chunks = comp.shape[3]
    kv_cols = NSA_KV_HEADS
    q_cols = N_HEADS

    def kv_spec(which):
        base = q_cols + which * kv_cols
        return pl.BlockSpec((seq, HEAD_DIM), lambda b, g, qi: (b, base + g))

    def cmp_spec(which):
        return pl.BlockSpec((None, None, None, n_chunks, HEAD_DIM),
                            lambda b, g, qi: (which, b, g, 0, 0))

    return pl.pallas_call(
        _nsa_kernel,
        grid=(batch, NSA_KV_HEADS, n_q),
        in_specs=[
            pl.BlockSpec((tq, HEADS_PER_GROUP * HEAD_DIM), lambda b, g, qi: (b * n_q + qi, g)),
            cmp_spec(0), cmp_spec(1),
            kv_spec(2), kv_spec(3), kv_spec(4), kv_spec(5),
            pl.BlockSpec((tq, LANES), lambda b, g, qi: (b * n_q + qi, g)),
        ],
        out_specs=pl.BlockSpec((tq, HEADS_PER_GROUP * HEAD_DIM), lambda b, g, qi: (b * n_q + qi, g)),
        out_shape=jax.ShapeDtypeStruct((t, N_HEADS * HEAD_DIM), BF16),
        scratch_shapes=[
            pltpu.VMEM((seq, 2 * HEAD_DIM), BF16),
            pltpu.VMEM((rows, 2 * HEAD_DIM), BF16),
            pltpu.VMEM((rows, LANES), F32),
            pltpu.VMEM((rows, LANES), F32),
            pltpu.VMEM((rows, HEAD_DIM), F32),
            pltpu.VMEM((rows, HEAD_DIM), F32),
            pltpu.VMEM((rows, tk), F32),
            pltpu.VMEM((rows, tk), F32),
        ],
        compiler_params=_params(3),
        name="nsa_attention",
    )(proj, comp, comp, proj, proj, proj, proj, gates)


def _sb_kernel(q_ref, k_ref, v_ref, o_ref, carry_ref, acc_ref, za_ref, zb_ref):
    tk = SB_TILE
    qi = pl.program_id(2)
    row = lax.broadcasted_iota(jnp.int32, (tk, tk), 0)
    col = lax.broadcasted_iota(jnp.int32, (tk, tk), 1)
    causal = col < row
    not_before = jnp.where(row >= col, 1.0, 0.0).astype(BF16)
    not_before2 = jnp.concatenate([not_before, not_before], axis=0)
    carry_ref[...] = jnp.zeros_like(carry_ref)
    acc_ref[...] = jnp.zeros_like(acc_ref)
    sign_bit = jnp.uint32(0x80000000)

    heads = range(carry_ref.shape[0])
    hs = [slice(h * HEAD_DIM, (h + 1) * HEAD_DIM) for h in heads]

    def scores(j, dst_ref):
        start = pl.multiple_of(j * tk, tk)
        for h in heads:
            dst_ref[h] = _dot_nt(q_ref[:, hs[h]], k_ref[pl.ds(start, tk), hs[h]])

    def consume(src_ref, j, diagonal):
        start = pl.multiple_of(j * tk, tk)
        z = [src_ref[h] for h in heads]
        neg_abs = [pltpu.bitcast(pltpu.bitcast(z[h], jnp.uint32) | sign_bit, F32) for h in heads]
        sp = [jnp.maximum(z[h], 0.0) + jnp.log(1.0 + jnp.exp2(neg_abs[h])) * LOG2_E for h in heads]
        if diagonal:
            sp = [jnp.where(causal, sp[h], 0.0) for h in heads]
        split = [jnp.concatenate(_split_bf16(sp[h]), axis=1) for h in heads]
        suffix = [_dot(split[h], not_before2) for h in heads]
        carry = [carry_ref[h] for h in heads]
        a = [jnp.exp2(z[h] - suffix[h] - jnp.concatenate([carry[h]] * (tk // LANES), axis=1))
             for h in heads]
        if diagonal:
            a = [jnp.where(causal, a[h], 0.0) for h in heads]
        for h in heads:
            acc_ref[h] += _dot(a[h].astype(BF16), v_ref[pl.ds(start, tk), hs[h]])
            carry_ref[h] = carry[h] + suffix[h][:, 0:1]

    scores(qi, za_ref)
    scores(jnp.maximum(qi - 1, 0), zb_ref)
    consume(za_ref, qi, True)
    n_pairs = jnp.maximum(qi - 1, 0) // 2

    def pair_body(i, c):
        j = qi - 1 - 2 * i
        scores(j - 1, za_ref)
        consume(zb_ref, j, False)
        scores(j - 2, zb_ref)
        consume(za_ref, j - 1, False)
        return c

    lax.fori_loop(0, n_pairs, pair_body, 0)
    j_next = qi - 1 - 2 * n_pairs

    @pl.when(j_next == 1)
    def _():
        scores(0, za_ref)
        consume(zb_ref, 1, False)
        consume(za_ref, 0, False)

    @pl.when(j_next == 0)
    def _():
        consume(zb_ref, 0, False)

    for h in heads:
        o_ref[:, h * HEAD_DIM:(h + 1) * HEAD_DIM] = acc_ref[h].astype(o_ref.dtype)


def _sb_attention(q, kv, batch, seq, n_heads=SB_HEADS):
    t = q.shape[0]
    tq = SB_TILE
    assert seq % tq == 0 and N_HEADS % n_heads == 0, (seq, tq, n_heads)
    n_q = seq // tq
    width = n_heads * HEAD_DIM
    n_hb = N_HEADS // n_heads
    return pl.pallas_call(
        _sb_kernel,
        grid=(batch, n_hb, n_q),
        in_specs=[
            pl.BlockSpec((tq, width), lambda b, h, qi: (b * n_q + qi, h)),
            pl.BlockSpec((seq, width), lambda b, h, qi: (b, h)),
            pl.BlockSpec((seq, width), lambda b, h, qi: (b, n_hb + h)),
        ],
        out_specs=pl.BlockSpec((tq, width), lambda b, h, qi: (b * n_q + qi, h)),
        out_shape=jax.ShapeDtypeStruct((t, N_HEADS * HEAD_DIM), BF16),
        scratch_shapes=[pltpu.VMEM((n_heads, tq, LANES), F32),
                        pltpu.VMEM((n_heads, tq, HEAD_DIM), F32),
                        pltpu.VMEM((n_heads, tq, SB_TILE), F32),
                        pltpu.VMEM((n_heads, tq, SB_TILE), F32)],
        compiler_params=_params(3),
        name="sb_attention",
    )(q, kv, kv)


def _rope_tables(seq):
    half = HEAD_DIM // 2
    inv = ROPE_THETA ** (-jnp.arange(half, dtype=F32) / half)
    ang = jnp.arange(seq, dtype=F32)[:, None] * inv[None, :]
    cos, sin = jnp.cos(ang), jnp.sin(ang)
    return jnp.concatenate([cos, cos], axis=-1), jnp.concatenate([-sin, sin], axis=-1)


def kernel(x, c, mod_w, mod_b, norm_g, ffn_w_in, ffn_w_out, a_w_in, a_gate_b, a_cmp_pe, a_cmp_w1,
           a_cmp_w2, a_w_out, b_w_q, b_w_out, kv_norm_g, kv_mod_w, kv_mod_b, kv_w):
    batch, seq, d = x.shape
    depth = mod_w.shape[0]
    n_a = a_w_in.shape[0]
    t = batch * seq
    hd = N_HEADS * HEAD_DIM
    kvd = NSA_KV_HEADS * HEAD_DIM
    x2d = x.reshape(t, d)

    c_pad = jnp.pad(c, ((0, 8 - batch), (0, 0)))
    mod = _modulation(c_pad, mod_w, mod_b[:, None, :])[:, :batch]
    mod = mod.reshape(depth, batch, 6, 1, d)
    kv_mod = _modulation(c_pad, kv_mod_w[None], kv_mod_b[None, None, :])[0, :batch]
    kv_mod = kv_mod.reshape(batch, 2, 1, d)
    cos, sin = _rope_tables(seq)
    ffn_w_in_bf16 = ffn_w_in.astype(BF16)
    ffn_w_out_bf16 = ffn_w_out.astype(BF16)

    kv = None
    for layer in range(depth):
        sh1, sc1, g1, sh2, sc2, g2 = (mod[layer, :, i] for i in range(6))
        gam = norm_g[layer][:, None, :]
        if layer < n_a:
            w_in = a_w_in[layer]
            n_main = hd + 6 * kvd
            proj, chunks = _project(x2d, seq, gam[0], sh1, sc1, w_in[:, :n_main].astype(BF16),
                                    mode="nsa", extra=(cos, sin), tn=2 * kvd)
            n_gate = 3 * HEADS_PER_GROUP
            w_gate = w_in[:, n_main:].reshape(d, NSA_KV_HEADS, n_gate)
            w_gate = jnp.pad(w_gate, ((0, 0), (0, 0), (0, LANES - n_gate))).reshape(d, NSA_KV_HEADS * LANES)
            b_gate = jnp.pad(a_gate_b[layer].reshape(NSA_KV_HEADS, n_gate),
                             ((0, 0), (0, LANES - n_gate))).reshape(1, NSA_KV_HEADS * LANES)
            gates = _project(x2d, seq, gam[0], sh1, sc1, w_gate.astype(BF16), mode="gate",
                             out_dtype=F32, extra=(b_gate,), tn=NSA_KV_HEADS * LANES)
            comp = _compress(chunks, a_cmp_w1[layer].astype(BF16), a_cmp_w2[layer].astype(BF16),
                             a_cmp_pe[layer].reshape(2, 1, CMP_BLOCK * HEAD_DIM))
            o = _nsa_attention(proj, comp, gates, batch, seq)
            x2d = _out_project(o, a_w_out[layer].astype(BF16), x2d, seq, g1, gam[1])
        else:
            jb = layer - n_a
            if kv is None:
                kv = _project(x2d, seq, kv_norm_g[None, :], kv_mod[:, 0], kv_mod[:, 1],
                              kv_w.astype(BF16))
            q = _project(x2d, seq, gam[0], sh1, sc1, b_w_q[jb].astype(BF16), out_scale=Q_SCALE)
            o = _sb_attention(q, kv, batch, seq)
            x2d = _out_project(o, b_w_out[jb].astype(BF16), x2d, seq, g1, gam[1])
        x2d = _ffn(x2d, seq, gam[2], sh2, sc2, ffn_w_in_bf16, ffn_w_out_bf16, layer, g2, gam[3])
    return x2d.reshape(batch, seq, d)
```

```python
import functools

import jax
import jax.numpy as jnp
import numpy as np
from jax import lax
from jax.experimental import pallas as pl
from jax.experimental.pallas import tpu as pltpu

N_HEADS = 16
HEAD_DIM = 128
NSA_KV_HEADS = 4
HEADS_PER_GROUP = N_HEADS // NSA_KV_HEADS
CMP_BLOCK = 32
CMP_STRIDE = 16
CMP_HIDDEN = 256
SEL_BLOCK = 64
SEL_TOPK = 16
WINDOW = 512
D_FF = 5632
ROPE_THETA = 10000.0
EPS = 1e-6
NEG = -1e30
FORCE = 1e9
ATTN_SCALE = HEAD_DIM ** -0.5
LOG2_E = float(np.log2(np.e))
Q_SCALE = ATTN_SCALE * LOG2_E
UNSELECTED = -(2.0 ** 100)

LANES = 128
BF16 = jnp.bfloat16
F32 = jnp.float32

ROW_TILE = 1024
OUT_ROW_TILE = 512
FFN_ROW_TILE = 1024
PROJ_COL_TILE = 2048
FF_TILE = 512
NSA_Q_TILE = 256
NSA_K_TILE = 512
SB_TILE = 256
SB_HEADS = 8
MOD_COL_TILE = 1024
VMEM_LIMIT = 56 * 1024 * 1024
FFN_VMEM_LIMIT = 62 * 1024 * 1024
NSA_PROJ_VMEM_LIMIT = 62 * 1024 * 1024

_NT = (((1,), (1,)), ((), ()))


def _params(n_axes, vmem_limit=VMEM_LIMIT):
    return pltpu.CompilerParams(
        dimension_semantics=("arbitrary",) * n_axes, vmem_limit_bytes=vmem_limit)


def _dot(a, b):
    return jnp.dot(a, b, preferred_element_type=F32)


def _dot_nt(a, b):
    return lax.dot_general(a, b, _NT, preferred_element_type=F32)


def _split_bf16(x):
    hi = x.astype(BF16)
    lo = (x - hi.astype(F32)).astype(BF16)
    return hi, lo


NORM_ROWS = 16
NORM_UNROLL = 8


def _normmod_store(x_ref, g_ref, sh_ref, sc_ref, h_ref):
    gain = g_ref[...] * (1.0 + sc_ref[...])
    shift = sh_ref[...]

    def body(c, carry):
        r0 = pl.multiple_of(c * NORM_ROWS, NORM_ROWS)
        x = x_ref[pl.ds(r0, NORM_ROWS), :]
        ms = jnp.mean(x * x, axis=-1, keepdims=True)
        h_ref[pl.ds(r0, NORM_ROWS), :] = (x * lax.rsqrt(ms + EPS) * gain + shift).astype(h_ref.dtype)
        return carry

    lax.fori_loop(0, x_ref.shape[0] // NORM_ROWS, body, 0, unroll=NORM_UNROLL)


def _residual_norm_inplace(out_ref, x_ref, gate_ref, g_ref):
    gain = gate_ref[...] * g_ref[...]
    rows = NORM_ROWS // 2
    for r0 in range(0, out_ref.shape[0], rows):
        y = out_ref[r0:r0 + rows, :]
        ms = jnp.mean(y * y, axis=-1, keepdims=True)
        out_ref[r0:r0 + rows, :] = x_ref[r0:r0 + rows, :] + y * lax.rsqrt(ms + EPS) * gain


def _mod_kernel(c_ref, w_ref, b_ref, o_ref):
    c = c_ref[...]
    ca = c * jax.nn.sigmoid(c)
    c_hi, c_lo = _split_bf16(ca)
    w_hi, w_lo = _split_bf16(w_ref[...])
    acc = _dot(c_hi, w_hi) + _dot(c_lo, w_hi) + _dot(c_hi, w_lo)
    o_ref[...] = acc + b_ref[...]


def _modulation(c_pad, w, b):
    n_layers, d, n = w.shape
    tn = MOD_COL_TILE
    assert n % tn == 0, (n, tn)
    return pl.pallas_call(
        _mod_kernel,
        grid=(n_layers, n // tn),
        in_specs=[
            pl.BlockSpec((8, d), lambda l, j: (0, 0)),
            pl.BlockSpec((None, d, tn), lambda l, j: (l, 0, j)),
            pl.BlockSpec((None, 1, tn), lambda l, j: (l, 0, j)),
        ],
        out_specs=pl.BlockSpec((None, 8, tn), lambda l, j: (l, 0, j)),
        out_shape=jax.ShapeDtypeStruct((n_layers, 8, n), F32),
        compiler_params=_params(2),
        name="modulation",
    )(c_pad, w, b)


def _proj_kernel(x_ref, g_ref, sh_ref, sc_ref, w_ref, *rest, mode, out_scale):
    if mode == "nsa":
        cos_ref, sin_ref, wg_ref, bg_ref, o_ref, chunk_ref, gates_ref, h_ref, relayout_ref = rest
    else:
        o_ref, h_ref = rest
    j = pl.program_id(1)

    @pl.when(j == 0)
    def _():
        _normmod_store(x_ref, g_ref, sh_ref, sc_ref, h_ref)
        if mode == "nsa":
            gates_ref[...] = jax.nn.sigmoid(_dot(h_ref[...], wg_ref[...]) + bg_ref[...])

    acc = _dot(h_ref[...], w_ref[...])
    if mode == "plain":
        o_ref[...] = (acc * out_scale).astype(o_ref.dtype)
    else:
        half = NSA_KV_HEADS
        is_q = j < 2
        scale = jnp.where(is_q, Q_SCALE, 1.0).astype(F32)
        cos_k = cos_ref[...] * scale
        sin_k = sin_ref[...] * scale
        cos_v = jnp.where(is_q, cos_k, 1.0)
        sin_v = jnp.where(is_q, sin_k, 0.0)

        def rotate(hh, cos, sin):
            xh = acc[:, hh * HEAD_DIM:(hh + 1) * HEAD_DIM]
            return xh * cos + pltpu.roll(xh, HEAD_DIM // 2, axis=1) * sin

        for hh in range(2 * half):
            rot = rotate(hh, cos_k, sin_k) if hh < half else rotate(hh, cos_v, sin_v)
            o_ref[:, hh * HEAD_DIM:(hh + 1) * HEAD_DIM] = rot.astype(o_ref.dtype)

        @pl.when(j == 2)
        def _():
            n_chunk_rows = chunk_ref.shape[2]
            for hh in range(2 * half):
                if hh < half:
                    relayout_ref[...] = rotate(hh, cos_ref[...], sin_ref[...])
                else:
                    relayout_ref[...] = acc[:, hh * HEAD_DIM:(hh + 1) * HEAD_DIM]
                for l in range(CMP_STRIDE):
                    rows_l = relayout_ref[pl.ds(l, n_chunk_rows, stride=CMP_STRIDE), :]
                    chunk_ref[hh // half, hh % half, :, l * HEAD_DIM:(l + 1) * HEAD_DIM] = (
                        rows_l.astype(chunk_ref.dtype))


def _project(x2d, seq, gamma, shift, scale, w, *, mode="plain", out_scale=1.0, extra=(),
             tn=PROJ_COL_TILE):
    t, d = x2d.shape
    n = w.shape[1]
    tm = min(ROW_TILE, seq)
    assert seq % tm == 0 and n % tn == 0, (seq, tm, n, tn)
    tiles_per_seq = seq // tm
    in_specs = [
        pl.BlockSpec((tm, d), lambda i, j: (i, 0)),
        pl.BlockSpec((1, d), lambda i, j: (0, 0)),
        pl.BlockSpec((None, 1, d), lambda i, j: (i // tiles_per_seq, 0, 0)),
        pl.BlockSpec((None, 1, d), lambda i, j: (i // tiles_per_seq, 0, 0)),
        pl.BlockSpec((d, tn), lambda i, j: (0, j)),
    ]
    out_specs = pl.BlockSpec((tm, tn), lambda i, j: (i, j))
    out_shape = jax.ShapeDtypeStruct((t, n), BF16)
    scratch_shapes = [pltpu.VMEM((tm, d), BF16)]
    if mode == "nsa":
        assert tn == 2 * NSA_KV_HEADS * HEAD_DIM and tm % CMP_STRIDE == 0, (tn, tm)
        n_gate = NSA_KV_HEADS * LANES
        in_specs += [pl.BlockSpec((tm, HEAD_DIM), lambda i, j: (i % tiles_per_seq, 0))] * 2
        in_specs += [pl.BlockSpec((d, n_gate), lambda i, j: (0, 0)),
                     pl.BlockSpec((1, n_gate), lambda i, j: (0, 0))]
        chunk_rows = tm // CMP_STRIDE
        width = CMP_STRIDE * HEAD_DIM
        out_specs = [out_specs,
                     pl.BlockSpec((2, None, NSA_KV_HEADS, chunk_rows, width),
                                  lambda i, j: (0, i // tiles_per_seq, 0, i % tiles_per_seq, 0)),
                     pl.BlockSpec((tm, n_gate), lambda i, j: (i, 0))]
        out_shape = [out_shape,
                     jax.ShapeDtypeStruct((2, t // seq, NSA_KV_HEADS, seq // CMP_STRIDE, width), BF16),
                     jax.ShapeDtypeStruct((t, n_gate), F32)]
        scratch_shapes += [pltpu.VMEM((tm, HEAD_DIM), F32)]
    return pl.pallas_call(
        functools.partial(_proj_kernel, mode=mode, out_scale=out_scale),
        grid=(t // tm, n // tn),
        in_specs=in_specs,
        out_specs=out_specs,
        out_shape=out_shape,
        scratch_shapes=scratch_shapes,
        compiler_params=_params(2, NSA_PROJ_VMEM_LIMIT if mode == "nsa" else VMEM_LIMIT),
        name="proj_" + mode,
    )(x2d, gamma, shift, scale, w, *extra)


def _outproj_kernel(o_ref, w_ref, x_ref, gate_ref, g_ref, out_ref):
    out_ref[...] = _dot(o_ref[...], w_ref[...])
    _residual_norm_inplace(out_ref, x_ref, gate_ref, g_ref)


def _out_project(o2d, w, x2d, seq, gate, gamma):
    t, d = x2d.shape
    k = o2d.shape[1]
    tm = min(OUT_ROW_TILE, seq)
    assert seq % tm == 0, (seq, tm)
    tiles_per_seq = seq // tm
    return pl.pallas_call(
        _outproj_kernel,
        grid=(t // tm,),
        in_specs=[
            pl.BlockSpec((tm, k), lambda i: (i, 0)),
            pl.BlockSpec((k, d), lambda i: (0, 0)),
            pl.BlockSpec((tm, d), lambda i: (i, 0)),
            pl.BlockSpec((None, 1, d), lambda i: (i // tiles_per_seq, 0, 0)),
            pl.BlockSpec((1, d), lambda i: (0, 0)),
        ],
        out_specs=pl.BlockSpec((tm, d), lambda i: (i, 0)),
        out_shape=jax.ShapeDtypeStruct((t, d), F32),
        compiler_params=_params(1),
        name="out_proj",
    )(o2d, w, x2d, gate, gamma)


def _ffn_kernel(x_ref, g_in_ref, sh_ref, sc_ref, wg_ref, wu_ref, wo_ref, gate_ref, g_out_ref,
                out_ref, h_ref):
    f = pl.program_id(1)

    @pl.when(f == 0)
    def _():
        _normmod_store(x_ref, g_in_ref, sh_ref, sc_ref, h_ref)
        out_ref[...] = jnp.zeros_like(out_ref)

    h = h_ref[...]
    gate = _dot(h, wg_ref[...])
    up = _dot(h, wu_ref[...])
    act = (gate * jax.nn.sigmoid(gate) * up).astype(BF16)
    out_ref[...] += _dot(act, wo_ref[...])

    @pl.when(f == pl.num_programs(1) - 1)
    def _():
        _residual_norm_inplace(out_ref, x_ref, gate_ref, g_out_ref)


def _ffn(x2d, seq, g_in, shift, scale, w_in, w_out, layer, gate, g_out):
    t, d = x2d.shape
    d_ff = w_out.shape[1]
    tm = min(FFN_ROW_TILE, seq)
    tf = FF_TILE
    assert d_ff % tf == 0 and seq % tm == 0, (d_ff, tf, seq, tm)
    n_f = d_ff // tf
    tiles_per_seq = seq // tm
    batch_vec = pl.BlockSpec((None, 1, d), lambda i, f: (i // tiles_per_seq, 0, 0))
    vec = pl.BlockSpec((1, d), lambda i, f: (0, 0))
    return pl.pallas_call(
        _ffn_kernel,
        grid=(t // tm, n_f),
        in_specs=[
            pl.BlockSpec((tm, d), lambda i, f: (i, 0)),
            vec, batch_vec, batch_vec,
            pl.BlockSpec((None, d, tf), lambda i, f: (layer, 0, f)),
            pl.BlockSpec((None, d, tf), lambda i, f: (layer, 0, n_f + f)),
            pl.BlockSpec((None, tf, d), lambda i, f: (layer, f, 0)),
            batch_vec, vec,
        ],
        out_specs=pl.BlockSpec((tm, d), lambda i, f: (i, 0)),
        out_shape=jax.ShapeDtypeStruct((t, d), F32),
        scratch_shapes=[pltpu.VMEM((tm, d), BF16)],
        compiler_params=_params(2, FFN_VMEM_LIMIT),
        name="ffn",
    )(x2d, g_in, shift, scale, w_in, w_in, w_out, gate, g_out)


def _gelu_tanh(x):
    return 0.5 * x * (1.0 + jnp.tanh(np.sqrt(2.0 / np.pi) * (x + 0.044715 * (x * x * x))))


def _compress_kernel(c_ref, w1_ref, w2_ref, pe_ref, o_ref):
    half = CMP_STRIDE * HEAD_DIM
    c = c_ref[...]
    n_chunks = c.shape[0]
    first = _dot(c, w1_ref[0:half, :])
    second = _dot(c, w1_ref[half:2 * half, :])
    pe = jnp.broadcast_to(pe_ref[...], (8, 2 * half)).astype(BF16)
    bias = _dot(pe, w1_ref[...])[0:1, :]
    hidden = first + pltpu.roll(second, n_chunks - 1, axis=0) + bias
    out = _dot(_gelu_tanh(hidden).astype(BF16), w2_ref[...])
    row = lax.broadcasted_iota(jnp.int32, out.shape, 0)
    o_ref[...] = jnp.where(row < n_chunks - 1, out, 0.0).astype(o_ref.dtype)


def _compress(chunks, w1, w2, pe):
    _, b, g, n_chunks, width = chunks.shape
    return pl.pallas_call(
        _compress_kernel,
        grid=(2, b, g),
        in_specs=[
            pl.BlockSpec((None, None, None, n_chunks, width), lambda w, bi, gi: (w, bi, gi, 0, 0)),
            pl.BlockSpec((None, 2 * width, CMP_HIDDEN), lambda w, bi, gi: (w, 0, 0)),
            pl.BlockSpec((None, CMP_HIDDEN, HEAD_DIM), lambda w, bi, gi: (w, 0, 0)),
            pl.BlockSpec((None, 1, 2 * width), lambda w, bi, gi: (w, 0, 0)),
        ],
        out_specs=pl.BlockSpec((None, None, None, n_chunks, HEAD_DIM),
                               lambda w, bi, gi: (w, bi, gi, 0, 0)),
        out_shape=jax.ShapeDtypeStruct((2, b, g, n_chunks, HEAD_DIM), BF16),
        compiler_params=_params(3),
        name="compress",
    )(chunks, w1, w2, pe)


def _nsa_kernel(q_ref, kc_ref, vc_ref, ks_ref, vs_ref, kw_ref, vw_ref, gate_ref, o_ref,
                kaug_ref, qaug_ref, m_ref, l_ref, acc_ref, side_ref, sa_ref, sb_ref):
    tq = q_ref.shape[0]
    tk = sa_ref.shape[1]
    r_heads = HEADS_PER_GROUP
    rows = r_heads * tq
    seq = ks_ref.shape[0]
    n_cmp = kc_ref.shape[0]
    qi = pl.program_id(2)
    t0 = qi * tq

    @pl.when(qi == 0)
    def _():
        kaug_ref[:, 0:HEAD_DIM] = ks_ref[...]
        row = lax.broadcasted_iota(jnp.int32, (seq, LANES), 0)
        lane = lax.broadcasted_iota(jnp.int32, (seq, LANES), 1)
        kaug_ref[:, HEAD_DIM:2 * HEAD_DIM] = jnp.where(
            (row // SEL_BLOCK) == lane, 1.0, 0.0).astype(BF16)

    for r in range(r_heads):
        qaug_ref[r * tq:(r + 1) * tq, 0:HEAD_DIM] = q_ref[:, r * HEAD_DIM:(r + 1) * HEAD_DIM]
    q_all = qaug_ref[:, 0:HEAD_DIM]
    t_row = t0 + (lax.broadcasted_iota(jnp.int32, (rows, 1), 0) & (tq - 1))
    gates = gate_ref[...]

    def gate_col(branch):
        return jnp.concatenate([gates[:, 3 * r + branch:3 * r + branch + 1] for r in range(r_heads)], axis=0)

    span = WINDOW + tq
    w_start = pl.multiple_of(jnp.maximum(t0 - WINDOW, 0), tq)
    s_c = _dot_nt(q_all, kc_ref[...])
    s_w = _dot_nt(q_all, kw_ref[pl.ds(w_start, span), :])

    n_idx = lax.broadcasted_iota(jnp.int32, (rows, n_cmp), 1)
    valid = (n_idx * CMP_STRIDE + (CMP_BLOCK - 1)) <= t_row
    any_valid = jnp.where(t_row >= CMP_BLOCK - 1, 1.0, 0.0).astype(F32)
    s_c = jnp.where(valid, s_c, NEG)
    e_c = jnp.exp2(s_c - jnp.max(s_c, axis=-1, keepdims=True))
    p_c = e_c / jnp.sum(e_c, axis=-1, keepdims=True) * any_valid
    o_cmp = _dot(p_c.astype(BF16), vc_ref[...])
    p_sum = p_c[0:tq]
    for r in range(1, r_heads):
        p_sum = p_sum + p_c[r * tq:(r + 1) * tq]

    n_w = WINDOW // tq
    w_row = lax.broadcasted_iota(jnp.int32, (tq, tq), 0)
    w_col = lax.broadcasted_iota(jnp.int32, (tq, tq), 1)
    bias_newer = jnp.where(w_col > w_row, 0.0, NEG).astype(F32)
    bias_causal = jnp.where(w_col <= w_row, 0.0, NEG).astype(F32)
    clipped = n_w - jnp.minimum(qi, n_w)
    bias_blocks = []
    for b in range(span // tq):
        e = b + clipped
        bias_blocks.append(jnp.where(e == 0, bias_newer,
                                     jnp.where(e == n_w, bias_causal,
                                               jnp.where(e < n_w, 0.0, NEG).astype(F32))))
    bias = jnp.concatenate(bias_blocks, axis=1)
    s_w = s_w + jnp.concatenate([bias] * r_heads, axis=0)
    e_w = jnp.exp2(s_w - jnp.max(s_w, axis=-1, keepdims=True))
    o_win = _dot(e_w.astype(BF16), vw_ref[pl.ds(w_start, span), :]) / jnp.sum(e_w, axis=-1, keepdims=True)
    side_ref[...] = gate_col(0) * o_cmp + gate_col(2) * o_win

    n_blocks = seq // SEL_BLOCK
    n_sel = min(SEL_TOPK, n_blocks)
    m_o = lax.broadcasted_iota(jnp.int32, (LANES, n_cmp), 0)
    n_o = lax.broadcasted_iota(jnp.int32, (LANES, n_cmp), 1)
    overlap_t = jnp.where(n_o * CMP_STRIDE < (m_o + 1) * SEL_BLOCK,
                          jnp.where(n_o * CMP_STRIDE + CMP_BLOCK > m_o * SEL_BLOCK, 1.0, 0.0),
                          0.0).astype(BF16)
    p_hi, p_lo = _split_bf16(p_sum)
    imp = (_dot_nt(overlap_t, p_hi) + _dot_nt(overlap_t, p_lo))[0:n_blocks]
    blk = lax.broadcasted_iota(jnp.int32, (n_blocks, tq), 0)
    t_s = t0 + lax.broadcasted_iota(jnp.int32, (n_blocks, tq), 1)
    cur = t_s // SEL_BLOCK
    forced = (blk == 0) | (blk == cur) | (blk == cur - 1)
    avail = blk * SEL_BLOCK <= t_s
    imp = jnp.where(forced, FORCE, jnp.where(avail, imp, -FORCE))
    sub = 8
    imp_v = [imp[v * sub:(v + 1) * sub] for v in range(n_blocks // sub)]
    blk_v = lax.broadcasted_iota(jnp.int32, (sub, tq), 0)
    rank_v = [jnp.zeros((sub, tq), F32) for _ in imp_v]
    for mp in range(n_blocks):
        other = imp[mp:mp + 1]
        for v in range(len(imp_v)):
            wins_ties = jnp.where(other >= imp_v[v], 1.0, 0.0)
            loses_ties = jnp.where(other > imp_v[v], 1.0, 0.0)
            if v * sub > mp:
                ahead = wins_ties
            elif (v + 1) * sub - 1 <= mp:
                ahead = loses_ties
            else:
                ahead = jnp.where(blk_v + v * sub > mp, wins_ties, loses_ties)
            rank_v[v] = rank_v[v] + ahead
    rank = jnp.concatenate(rank_v, axis=0)
    mask_t = jnp.where(rank < n_sel, 0.0, UNSELECTED)
    mask_t = jnp.concatenate([mask_t, jnp.zeros((LANES - n_blocks, tq), F32)], axis=0)
    mask_feat = mask_t.T.astype(BF16)
    for r in range(r_heads):
        qaug_ref[r * tq:(r + 1) * tq, HEAD_DIM:2 * HEAD_DIM] = mask_feat

    m_ref[...] = jnp.full_like(m_ref, NEG)
    l_ref[...] = jnp.zeros_like(l_ref)
    acc_ref[...] = jnp.zeros_like(acc_ref)
    def scores(j, dst_ref):
        start = pl.multiple_of(j * tk, tk)
        dst_ref[...] = _dot_nt(qaug_ref[...], kaug_ref[pl.ds(start, tk), :])

    def consume(src_ref, j, causal):
        start = pl.multiple_of(j * tk, tk)
        s = src_ref[...]
        if causal:
            kp = start + lax.broadcasted_iota(jnp.int32, (rows, tk), 1)
            s = jnp.where(kp <= t_row, s, NEG)
        m_old = m_ref[...]
        m_new = jnp.maximum(m_old, jnp.max(s, axis=-1, keepdims=True))
        alpha = jnp.exp2(m_old - m_new)
        p = jnp.exp2(s - m_new[:, 0:1])
        l_ref[...] = alpha * l_ref[...] + jnp.sum(p, axis=-1, keepdims=True)
        acc_ref[...] = alpha * acc_ref[...] + _dot(p.astype(BF16), vs_ref[pl.ds(start, tk), :])
        m_ref[...] = m_new

    n_full = t0 // tk
    n_pairs = n_full // 2
    buf_a = sa_ref
    buf_b = sb_ref
    scores(0, buf_a)

    def pair_body(i, carry):
        j = 2 * i
        scores(j + 1, buf_b)
        consume(buf_a, j, False)
        scores(j + 2, buf_a)
        consume(buf_b, j + 1, False)
        return carry

    lax.fori_loop(0, n_pairs, pair_body, 0)
    j_next = 2 * n_pairs

    @pl.when(j_next < n_full)
    def _():
        scores(j_next + 1, buf_b)
        consume(buf_a, j_next, False)
        consume(buf_b, j_next + 1, True)

    @pl.when(j_next == n_full)
    def _():
        consume(buf_a, j_next, True)

    out = side_ref[...] + gate_col(1) * (acc_ref[...] / l_ref[...])
    for r in range(r_heads):
        o_ref[:, r * HEAD_DIM:(r + 1) * HEAD_DIM] = out[r * tq:(r + 1) * tq].astype(o_ref.dtype)


def _nsa_attention(proj, comp, gates, batch, seq):
    t = proj.shape[0]
    tq, tk = NSA_Q_TILE, NSA_K_TILE
    assert seq % tk == 0 and tk % tq == 0 and seq % SEL_BLOCK == 0, (seq, tq, tk)
    assert WINDOW % tq == 0, (WINDOW, tq)
    n_q = seq // tq
    rows = HEADS_PER_GROUP * tq
    n_chunks = comp.shape[3]
    kv_cols = NSA_KV_HEADS
    q_cols = N_HEADS

    def kv_spec(which):
        base = q_cols + which * kv_cols
        return pl.BlockSpec((seq, HEAD_DIM), lambda b, g, qi: (b, base + g))

    def cmp_spec(which):
        return pl.BlockSpec((None, None, None, n_chunks, HEAD_DIM),
                            lambda b, g, qi: (which, b, g, 0, 0))

    return pl.pallas_call(
        _nsa_kernel,
        grid=(batch, NSA_KV_HEADS, n_q),
        in_specs=[
            pl.BlockSpec((tq, HEADS_PER_GROUP * HEAD_DIM), lambda b, g, qi: (b * n_q + qi, g)),
            cmp_spec(0), cmp_spec(1),
            kv_spec(2), kv_spec(3), kv_spec(4), kv_spec(5),
            pl.BlockSpec((tq, LANES), lambda b, g, qi: (b * n_q + qi, g)),
        ],
        out_specs=pl.BlockSpec((tq, HEADS_PER_GROUP * HEAD_DIM), lambda b, g, qi: (b * n_q + qi, g)),
        out_shape=jax.ShapeDtypeStruct((t, N_HEADS * HEAD_DIM), BF16),
        scratch_shapes=[
            pltpu.VMEM((seq, 2 * HEAD_DIM), BF16),
            pltpu.VMEM((rows, 2 * HEAD_DIM), BF16),
            pltpu.VMEM((rows, LANES), F32),
            pltpu.VMEM((rows, LANES), F32),
            pltpu.VMEM((rows, HEAD_DIM), F32),
            pltpu.VMEM((rows, HEAD_DIM), F32),
            pltpu.VMEM((rows, tk), F32),
            pltpu.VMEM((rows, tk), F32),
        ],
        compiler_params=_params(3),
        name="nsa_attention",
    )(proj, comp, comp, proj, proj, proj, proj, gates)


def _sb_kernel(q_ref, k_ref, v_ref, o_ref, carry_ref, acc_ref, za_ref, zb_ref):
    tk = SB_TILE
    qi = pl.program_id(2)
    row = lax.broadcasted_iota(jnp.int32, (tk, tk), 0)
    col = lax.broadcasted_iota(jnp.int32, (tk, tk), 1)
    causal = col < row
    not_before = jnp.where(row >= col, 1.0, 0.0).astype(BF16)
    not_before2 = jnp.concatenate([not_before, not_before], axis=0)
    carry_ref[...] = jnp.zeros_like(carry_ref)
    acc_ref[...] = jnp.zeros_like(acc_ref)
    sign_bit = jnp.uint32(0x80000000)

    heads = range(carry_ref.shape[0])
    hs = [slice(h * HEAD_DIM, (h + 1) * HEAD_DIM) for h in heads]

    def scores(j, dst_ref):
        start = pl.multiple_of(j * tk, tk)
        for h in heads:
            dst_ref[h] = _dot_nt(q_ref[:, hs[h]], k_ref[pl.ds(start, tk), hs[h]])

    def consume(src_ref, j, diagonal):
        start = pl.multiple_of(j * tk, tk)
        z = [src_ref[h] for h in heads]
        neg_abs = [pltpu.bitcast(pltpu.bitcast(z[h], jnp.uint32) | sign_bit, F32) for h in heads]
        sp = [jnp.maximum(z[h], 0.0) + jnp.log(1.0 + jnp.exp2(neg_abs[h])) * LOG2_E for h in heads]
        if diagonal:
            sp = [jnp.where(causal, sp[h], 0.0) for h in heads]
        split = [jnp.concatenate(_split_bf16(sp[h]), axis=1) for h in heads]
        suffix = [_dot(split[h], not_before2) for h in heads]
        carry = [carry_ref[h] for h in heads]
        a = [jnp.exp2(z[h] - suffix[h] - jnp.concatenate([carry[h]] * (tk // LANES), axis=1))
             for h in heads]
        if diagonal:
            a = [jnp.where(causal, a[h], 0.0) for h in heads]
        for h in heads:
            acc_ref[h] += _dot(a[h].astype(BF16), v_ref[pl.ds(start, tk), hs[h]])
            carry_ref[h] = carry[h] + suffix[h][:, 0:1]

    scores(qi, za_ref)
    scores(jnp.maximum(qi - 1, 0), zb_ref)
    consume(za_ref, qi, True)
    n_pairs = jnp.maximum(qi - 1, 0) // 2

    def pair_body(i, c):
        j = qi - 1 - 2 * i
        scores(j - 1, za_ref)
        consume(zb_ref, j, False)
        scores(j - 2, zb_ref)
        consume(za_ref, j - 1, False)
        return c

    lax.fori_loop(0, n_pairs, pair_body, 0)
    j_next = qi - 1 - 2 * n_pairs

    @pl.when(j_next == 1)
    def _():
        scores(0, za_ref)
        consume(zb_ref, 1, False)
        consume(za_ref, 0, False)

    @pl.when(j_next == 0)
    def _():
        consume(zb_ref, 0, False)

    for h in heads:
        o_ref[:, h * HEAD_DIM:(h + 1) * HEAD_DIM] = acc_ref[h].astype(o_ref.dtype)


def _sb_attention(q, kv, batch, seq):
    t = q.shape[0]
    tq = SB_TILE
    n_heads = SB_HEADS
    assert seq % tq == 0 and N_HEADS % n_heads == 0, (seq, tq, n_heads)
    n_q = seq // tq
    width = n_heads * HEAD_DIM
    n_hb = N_HEADS // n_heads
    return pl.pallas_call(
        _sb_kernel,
        grid=(batch, n_hb, n_q),
        in_specs=[
            pl.BlockSpec((tq, width), lambda b, h, qi: (b * n_q + qi, h)),
            pl.BlockSpec((seq, width), lambda b, h, qi: (b, h)),
            pl.BlockSpec((seq, width), lambda b, h, qi: (b, n_hb + h)),
        ],
        out_specs=pl.BlockSpec((tq, width), lambda b, h, qi: (b * n_q + qi, h)),
        out_shape=jax.ShapeDtypeStruct((t, N_HEADS * HEAD_DIM), BF16),
        scratch_shapes=[pltpu.VMEM((n_heads, tq, LANES), F32),
                        pltpu.VMEM((n_heads, tq, HEAD_DIM), F32),
                        pltpu.VMEM((n_heads, tq, SB_TILE), F32),
                        pltpu.VMEM((n_heads, tq, SB_TILE), F32)],
        compiler_params=_params(3),
        name="sb_attention",
    )(q, kv, kv)


def _rope_tables(seq):
    half = HEAD_DIM // 2
    inv = ROPE_THETA ** (-jnp.arange(half, dtype=F32) / half)
    ang = jnp.arange(seq, dtype=F32)[:, None] * inv[None, :]
    cos, sin = jnp.cos(ang), jnp.sin(ang)
    return jnp.concatenate([cos, cos], axis=-1), jnp.concatenate([-sin, sin], axis=-1)


def kernel(x, c, mod_w, mod_b, norm_g, ffn_w_in, ffn_w_out, a_w_in, a_gate_b, a_cmp_pe, a_cmp_w1,
           a_cmp_w2, a_w_out, b_w_q, b_w_out, kv_norm_g, kv_mod_w, kv_mod_b, kv_w):
    batch, seq, d = x.shape
    depth = mod_w.shape[0]
    n_a = a_w_in.shape[0]
    t = batch * seq
    hd = N_HEADS * HEAD_DIM
    kvd = NSA_KV_HEADS * HEAD_DIM
    x2d = x.reshape(t, d)

    c_pad = jnp.pad(c, ((0, 8 - batch), (0, 0)))
    mod = _modulation(c_pad, mod_w, mod_b[:, None, :])[:, :batch]
    mod = mod.reshape(depth, batch, 6, 1, d)
    kv_mod = _modulation(c_pad, kv_mod_w[None], kv_mod_b[None, None, :])[0, :batch]
    kv_mod = kv_mod.reshape(batch, 2, 1, d)
    cos, sin = _rope_tables(seq)
    ffn_w_in_bf16 = ffn_w_in.astype(BF16)
    ffn_w_out_bf16 = ffn_w_out.astype(BF16)

    kv = None
    for layer in range(depth):
        sh1, sc1, g1, sh2, sc2, g2 = (mod[layer, :, i] for i in range(6))
        gam = norm_g[layer][:, None, :]
        if layer < n_a:
            w_in = a_w_in[layer]
            n_main = hd + 6 * kvd
            n_gate = 3 * HEADS_PER_GROUP
            w_gate = w_in[:, n_main:].reshape(d, NSA_KV_HEADS, n_gate)
            w_gate = jnp.pad(w_gate, ((0, 0), (0, 0), (0, LANES - n_gate))).reshape(d, NSA_KV_HEADS * LANES)
            b_gate = jnp.pad(a_gate_b[layer].reshape(NSA_KV_HEADS, n_gate),
                             ((0, 0), (0, LANES - n_gate))).reshape(1, NSA_KV_HEADS * LANES)
            proj, chunks, gates = _project(
                x2d, seq, gam[0], sh1, sc1, w_in[:, :n_main].astype(BF16), mode="nsa",
                extra=(cos, sin, w_gate.astype(BF16), b_gate), tn=2 * kvd)
            comp = _compress(chunks, a_cmp_w1[layer].astype(BF16), a_cmp_w2[layer].astype(BF16),
                             a_cmp_pe[layer].reshape(2, 1, CMP_BLOCK * HEAD_DIM))
            o = _nsa_attention(proj, comp, gates, batch, seq)
            x2d = _out_project(o, a_w_out[layer].astype(BF16), x2d, seq, g1, gam[1])
        else:
            jb = layer - n_a
            if kv is None:
                kv = _project(x2d, seq, kv_norm_g[None, :], kv_mod[:, 0], kv_mod[:, 1],
                              kv_w.astype(BF16))
            q = _project(x2d, seq, gam[0], sh1, sc1, b_w_q[jb].astype(BF16), out_scale=Q_SCALE)
            o = _sb_attention(q, kv, batch, seq)
            x2d = _out_project(o, b_w_out[jb].astype(BF16), x2d, seq, g1, gam[1])
        x2d = _ffn(x2d, seq, gam[2], sh2, sc2, ffn_w_in_bf16, ffn_w_out_bf16, layer, g2, gam[3])
    return x2d.reshape(batch, seq, d)
```

```python
import functools

import jax
import jax.numpy as jnp
import numpy as np
from jax import lax
from jax.experimental import pallas as pl
from jax.experimental.pallas import tpu as pltpu

N_HEADS = 16
HEAD_DIM = 128
NSA_KV_HEADS = 4
HEADS_PER_GROUP = N_HEADS // NSA_KV_HEADS
CMP_BLOCK = 32
CMP_STRIDE = 16
CMP_HIDDEN = 256
SEL_BLOCK = 64
SEL_TOPK = 16
WINDOW = 512
D_FF = 5632
ROPE_THETA = 10000.0
EPS = 1e-6
NEG = -1e30
FORCE = 1e9
ATTN_SCALE = HEAD_DIM ** -0.5
LOG2_E = float(np.log2(np.e))
Q_SCALE = ATTN_SCALE * LOG2_E
UNSELECTED = -(2.0 ** 100)

LANES = 128
BF16 = jnp.bfloat16
F32 = jnp.float32

ROW_TILE = 1024
OUT_ROW_TILE = 512
FFN_ROW_TILE = 1024
PROJ_COL_TILE = 2048
FF_TILE = 512
NSA_Q_TILE = 256
NSA_K_TILE = 512
SB_TILE = 256
SB_HEADS = 8
MOD_COL_TILE = 1024
VMEM_LIMIT = 56 * 1024 * 1024
FFN_VMEM_LIMIT = 62 * 1024 * 1024
NSA_PROJ_VMEM_LIMIT = 62 * 1024 * 1024

_NT = (((1,), (1,)), ((), ()))


def _params(n_axes, vmem_limit=VMEM_LIMIT):
    return pltpu.CompilerParams(
        dimension_semantics=("arbitrary",) * n_axes, vmem_limit_bytes=vmem_limit)


def _dot(a, b):
    return jnp.dot(a, b, preferred_element_type=F32)


def _dot_nt(a, b):
    return lax.dot_general(a, b, _NT, preferred_element_type=F32)


def _split_bf16(x):
    hi = x.astype(BF16)
    lo = (x - hi.astype(F32)).astype(BF16)
    return hi, lo


NORM_ROWS = 16
NORM_UNROLL = 16


def _normmod_store(x_ref, g_ref, sh_ref, sc_ref, h_ref):
    gain = g_ref[...] * (1.0 + sc_ref[...])
    shift = sh_ref[...]

    def body(c, carry):
        r0 = pl.multiple_of(c * NORM_ROWS, NORM_ROWS)
        x = x_ref[pl.ds(r0, NORM_ROWS), :]
        ms = jnp.mean(x * x, axis=-1, keepdims=True)
        h_ref[pl.ds(r0, NORM_ROWS), :] = (x * lax.rsqrt(ms + EPS) * gain + shift).astype(h_ref.dtype)
        return carry

    lax.fori_loop(0, x_ref.shape[0] // NORM_ROWS, body, 0, unroll=NORM_UNROLL)


def _residual_norm_inplace(out_ref, x_ref, gate_ref, g_ref):
    gain = gate_ref[...] * g_ref[...]
    rows = NORM_ROWS // 2
    for r0 in range(0, out_ref.shape[0], rows):
        y = out_ref[r0:r0 + rows, :]
        ms = jnp.mean(y * y, axis=-1, keepdims=True)
        out_ref[r0:r0 + rows, :] = x_ref[r0:r0 + rows, :] + y * lax.rsqrt(ms + EPS) * gain


def _mod_kernel(c_ref, w_ref, b_ref, o_ref):
    c = c_ref[...]
    ca = c * jax.nn.sigmoid(c)
    c_hi, c_lo = _split_bf16(ca)
    w_hi, w_lo = _split_bf16(w_ref[...])
    acc = _dot(c_hi, w_hi) + _dot(c_lo, w_hi) + _dot(c_hi, w_lo)
    o_ref[...] = acc + b_ref[...]


def _modulation(c_pad, w, b):
    n_layers, d, n = w.shape
    tn = MOD_COL_TILE
    assert n % tn == 0, (n, tn)
    return pl.pallas_call(
        _mod_kernel,
        grid=(n_layers, n // tn),
        in_specs=[
            pl.BlockSpec((8, d), lambda l, j: (0, 0)),
            pl.BlockSpec((None, d, tn), lambda l, j: (l, 0, j)),
            pl.BlockSpec((None, 1, tn), lambda l, j: (l, 0, j)),
        ],
        out_specs=pl.BlockSpec((None, 8, tn), lambda l, j: (l, 0, j)),
        out_shape=jax.ShapeDtypeStruct((n_layers, 8, n), F32),
        compiler_params=_params(2),
        name="modulation",
    )(c_pad, w, b)


def _proj_kernel(x_ref, g_ref, sh_ref, sc_ref, w_ref, *rest, mode, out_scale):
    if mode == "nsa":
        cos_ref, sin_ref, wg_ref, bg_ref, o_ref, chunk_ref, gates_ref, h_ref, relayout_ref = rest
    else:
        o_ref, h_ref = rest
    j = pl.program_id(1)

    @pl.when(j == 0)
    def _():
        _normmod_store(x_ref, g_ref, sh_ref, sc_ref, h_ref)
        if mode == "nsa":
            gates_ref[...] = jax.nn.sigmoid(_dot(h_ref[...], wg_ref[...]) + bg_ref[...])

    acc = _dot(h_ref[...], w_ref[...])
    if mode == "plain":
        o_ref[...] = (acc * out_scale).astype(o_ref.dtype)
    else:
        half = NSA_KV_HEADS
        is_q = j < 2
        scale = jnp.where(is_q, Q_SCALE, 1.0).astype(F32)
        cos_k = cos_ref[...] * scale
        sin_k = sin_ref[...] * scale
        cos_v = jnp.where(is_q, cos_k, 1.0)
        sin_v = jnp.where(is_q, sin_k, 0.0)

        def rotate(hh, cos, sin):
            xh = acc[:, hh * HEAD_DIM:(hh + 1) * HEAD_DIM]
            return xh * cos + pltpu.roll(xh, HEAD_DIM // 2, axis=1) * sin

        for hh in range(2 * half):
            rot = rotate(hh, cos_k, sin_k) if hh < half else rotate(hh, cos_v, sin_v)
            o_ref[:, hh * HEAD_DIM:(hh + 1) * HEAD_DIM] = rot.astype(o_ref.dtype)

        @pl.when(j == 2)
        def _():
            n_chunk_rows = chunk_ref.shape[2]
            for hh in range(2 * half):
                if hh < half:
                    relayout_ref[...] = rotate(hh, cos_ref[...], sin_ref[...])
                else:
                    relayout_ref[...] = acc[:, hh * HEAD_DIM:(hh + 1) * HEAD_DIM]
                for l in range(CMP_STRIDE):
                    rows_l = relayout_ref[pl.ds(l, n_chunk_rows, stride=CMP_STRIDE), :]
                    chunk_ref[hh // half, hh % half, :, l * HEAD_DIM:(l + 1) * HEAD_DIM] = (
                        rows_l.astype(chunk_ref.dtype))


def _project(x2d, seq, gamma, shift, scale, w, *, mode="plain", out_scale=1.0, extra=(),
             tn=PROJ_COL_TILE):
    t, d = x2d.shape
    n = w.shape[1]
    tm = min(ROW_TILE, seq)
    assert seq % tm == 0 and n % tn == 0, (seq, tm, n, tn)
    tiles_per_seq = seq // tm
    in_specs = [
        pl.BlockSpec((tm, d), lambda i, j: (i, 0)),
        pl.BlockSpec((1, d), lambda i, j: (0, 0)),
        pl.BlockSpec((None, 1, d), lambda i, j: (i // tiles_per_seq, 0, 0)),
        pl.BlockSpec((None, 1, d), lambda i, j: (i // tiles_per_seq, 0, 0)),
        pl.BlockSpec((d, tn), lambda i, j: (0, j)),
    ]
    out_specs = pl.BlockSpec((tm, tn), lambda i, j: (i, j))
    out_shape = jax.ShapeDtypeStruct((t, n), BF16)
    scratch_shapes = [pltpu.VMEM((tm, d), BF16)]
    if mode == "nsa":
        assert tn == 2 * NSA_KV_HEADS * HEAD_DIM and tm % CMP_STRIDE == 0, (tn, tm)
        n_gate = NSA_KV_HEADS * LANES
        in_specs += [pl.BlockSpec((tm, HEAD_DIM), lambda i, j: (i % tiles_per_seq, 0))] * 2
        in_specs += [pl.BlockSpec((d, n_gate), lambda i, j: (0, 0)),
                     pl.BlockSpec((1, n_gate), lambda i, j: (0, 0))]
        chunk_rows = tm // CMP_STRIDE
        width = CMP_STRIDE * HEAD_DIM
        out_specs = [out_specs,
                     pl.BlockSpec((2, None, NSA_KV_HEADS, chunk_rows, width),
                                  lambda i, j: (0, i // tiles_per_seq, 0, i % tiles_per_seq, 0)),
                     pl.BlockSpec((tm, n_gate), lambda i, j: (i, 0))]
        out_shape = [out_shape,
                     jax.ShapeDtypeStruct((2, t // seq, NSA_KV_HEADS, seq // CMP_STRIDE, width), BF16),
                     jax.ShapeDtypeStruct((t, n_gate), F32)]
        scratch_shapes += [pltpu.VMEM((tm, HEAD_DIM), F32)]
    return pl.pallas_call(
        functools.partial(_proj_kernel, mode=mode, out_scale=out_scale),
        grid=(t // tm, n // tn),
        in_specs=in_specs,
        out_specs=out_specs,
        out_shape=out_shape,
        scratch_shapes=scratch_shapes,
        compiler_params=_params(2, NSA_PROJ_VMEM_LIMIT if mode == "nsa" else VMEM_LIMIT),
        name="proj_" + mode,
    )(x2d, gamma, shift, scale, w, *extra)


def _outproj_kernel(o_ref, w_ref, x_ref, gate_ref, g_ref, out_ref):
    out_ref[...] = _dot(o_ref[...], w_ref[...])
    _residual_norm_inplace(out_ref, x_ref, gate_ref, g_ref)


def _out_project(o2d, w, x2d, seq, gate, gamma):
    t, d = x2d.shape
    k = o2d.shape[1]
    tm = min(OUT_ROW_TILE, seq)
    assert seq % tm == 0, (seq, tm)
    tiles_per_seq = seq // tm
    return pl.pallas_call(
        _outproj_kernel,
        grid=(t // tm,),
        in_specs=[
            pl.BlockSpec((tm, k), lambda i: (i, 0)),
            pl.BlockSpec((k, d), lambda i: (0, 0)),
            pl.BlockSpec((tm, d), lambda i: (i, 0)),
            pl.BlockSpec((None, 1, d), lambda i: (i // tiles_per_seq, 0, 0)),
            pl.BlockSpec((1, d), lambda i: (0, 0)),
        ],
        out_specs=pl.BlockSpec((tm, d), lambda i: (i, 0)),
        out_shape=jax.ShapeDtypeStruct((t, d), F32),
        compiler_params=_params(1),
        name="out_proj",
    )(o2d, w, x2d, gate, gamma)


def _ffn_kernel(x_ref, g_in_ref, sh_ref, sc_ref, wg_ref, wu_ref, wo_ref, gate_ref, g_out_ref,
                out_ref, h_ref):
    f = pl.program_id(1)

    @pl.when(f == 0)
    def _():
        _normmod_store(x_ref, g_in_ref, sh_ref, sc_ref, h_ref)
        out_ref[...] = jnp.zeros_like(out_ref)

    h = h_ref[...]
    gate = _dot(h, wg_ref[...])
    up = _dot(h, wu_ref[...])
    act = (gate * jax.nn.sigmoid(gate) * up).astype(BF16)
    out_ref[...] += _dot(act, wo_ref[...])

    @pl.when(f == pl.num_programs(1) - 1)
    def _():
        _residual_norm_inplace(out_ref, x_ref, gate_ref, g_out_ref)


def _ffn(x2d, seq, g_in, shift, scale, w_in, w_out, layer, gate, g_out):
    t, d = x2d.shape
    d_ff = w_out.shape[1]
    tm = min(FFN_ROW_TILE, seq)
    tf = FF_TILE
    assert d_ff % tf == 0 and seq % tm == 0, (d_ff, tf, seq, tm)
    n_f = d_ff // tf
    tiles_per_seq = seq // tm
    batch_vec = pl.BlockSpec((None, 1, d), lambda i, f: (i // tiles_per_seq, 0, 0))
    vec = pl.BlockSpec((1, d), lambda i, f: (0, 0))
    return pl.pallas_call(
        _ffn_kernel,
        grid=(t // tm, n_f),
        in_specs=[
            pl.BlockSpec((tm, d), lambda i, f: (i, 0)),
            vec, batch_vec, batch_vec,
            pl.BlockSpec((None, d, tf), lambda i, f: (layer, 0, f)),
            pl.BlockSpec((None, d, tf), lambda i, f: (layer, 0, n_f + f)),
            pl.BlockSpec((None, tf, d), lambda i, f: (layer, f, 0)),
            batch_vec, vec,
        ],
        out_specs=pl.BlockSpec((tm, d), lambda i, f: (i, 0)),
        out_shape=jax.ShapeDtypeStruct((t, d), F32),
        scratch_shapes=[pltpu.VMEM((tm, d), BF16)],
        compiler_params=_params(2, FFN_VMEM_LIMIT),
        name="ffn",
    )(x2d, g_in, shift, scale, w_in, w_in, w_out, gate, g_out)


def _gelu_tanh(x):
    return 0.5 * x * (1.0 + jnp.tanh(np.sqrt(2.0 / np.pi) * (x + 0.044715 * (x * x * x))))


def _compress_kernel(c_ref, w1_ref, w2_ref, pe_ref, o_ref):
    half = CMP_STRIDE * HEAD_DIM
    c = c_ref[...]
    n_chunks = c.shape[0]
    first = _dot(c, w1_ref[0:half, :])
    second = _dot(c, w1_ref[half:2 * half, :])
    pe = jnp.broadcast_to(pe_ref[...], (8, 2 * half)).astype(BF16)
    bias = _dot(pe, w1_ref[...])[0:1, :]
    hidden = first + pltpu.roll(second, n_chunks - 1, axis=0) + bias
    out = _dot(_gelu_tanh(hidden).astype(BF16), w2_ref[...])
    row = lax.broadcasted_iota(jnp.int32, out.shape, 0)
    o_ref[...] = jnp.where(row < n_chunks - 1, out, 0.0).astype(o_ref.dtype)


def _compress(chunks, w1, w2, pe):
    _, b, g, n_chunks, width = chunks.shape
    return pl.pallas_call(
        _compress_kernel,
        grid=(2, b, g),
        in_specs=[
            pl.BlockSpec((None, None, None, n_chunks, width), lambda w, bi, gi: (w, bi, gi, 0, 0)),
            pl.BlockSpec((None, 2 * width, CMP_HIDDEN), lambda w, bi, gi: (w, 0, 0)),
            pl.BlockSpec((None, CMP_HIDDEN, HEAD_DIM), lambda w, bi, gi: (w, 0, 0)),
            pl.BlockSpec((None, 1, 2 * width), lambda w, bi, gi: (w, 0, 0)),
        ],
        out_specs=pl.BlockSpec((None, None, None, n_chunks, HEAD_DIM),
                               lambda w, bi, gi: (w, bi, gi, 0, 0)),
        out_shape=jax.ShapeDtypeStruct((2, b, g, n_chunks, HEAD_DIM), BF16),
        compiler_params=_params(3),
        name="compress",
    )(chunks, w1, w2, pe)


def _nsa_kernel(q_ref, kc_ref, vc_ref, ks_ref, vs_ref, kw_ref, vw_ref, gate_ref, o_ref,
                kaug_ref, qaug_ref, m_ref, l_ref, acc_ref, side_ref, sa_ref, sb_ref):
    tq = q_ref.shape[0]
    tk = sa_ref.shape[1]
    r_heads = HEADS_PER_GROUP
    rows = r_heads * tq
    seq = ks_ref.shape[0]
    n_cmp = kc_ref.shape[0]
    qi = pl.program_id(2)
    t0 = qi * tq

    @pl.when(qi == 0)
    def _():
        kaug_ref[:, 0:HEAD_DIM] = ks_ref[...]
        row = lax.broadcasted_iota(jnp.int32, (seq, LANES), 0)
        lane = lax.broadcasted_iota(jnp.int32, (seq, LANES), 1)
        kaug_ref[:, HEAD_DIM:2 * HEAD_DIM] = jnp.where(
            (row // SEL_BLOCK) == lane, 1.0, 0.0).astype(BF16)

    for r in range(r_heads):
        qaug_ref[r * tq:(r + 1) * tq, 0:HEAD_DIM] = q_ref[:, r * HEAD_DIM:(r + 1) * HEAD_DIM]
    q_all = qaug_ref[:, 0:HEAD_DIM]
    t_row = t0 + (lax.broadcasted_iota(jnp.int32, (rows, 1), 0) & (tq - 1))
    gates = gate_ref[...]

    def gate_col(branch):
        return jnp.concatenate([gates[:, 3 * r + branch:3 * r + branch + 1] for r in range(r_heads)], axis=0)

    span = WINDOW + tq
    w_start = pl.multiple_of(jnp.maximum(t0 - WINDOW, 0), tq)
    s_c = _dot_nt(q_all, kc_ref[...])
    s_w = _dot_nt(q_all, kw_ref[pl.ds(w_start, span), :])

    n_idx = lax.broadcasted_iota(jnp.int32, (rows, n_cmp), 1)
    valid = (n_idx * CMP_STRIDE + (CMP_BLOCK - 1)) <= t_row
    any_valid = jnp.where(t_row >= CMP_BLOCK - 1, 1.0, 0.0).astype(F32)
    s_c = jnp.where(valid, s_c, NEG)
    e_c = jnp.exp2(s_c - jnp.max(s_c, axis=-1, keepdims=True))
    p_c = e_c / jnp.sum(e_c, axis=-1, keepdims=True) * any_valid
    o_cmp = _dot(p_c.astype(BF16), vc_ref[...])
    p_sum = p_c[0:tq]
    for r in range(1, r_heads):
        p_sum = p_sum + p_c[r * tq:(r + 1) * tq]

    n_w = WINDOW // tq
    w_row = lax.broadcasted_iota(jnp.int32, (tq, tq), 0)
    w_col = lax.broadcasted_iota(jnp.int32, (tq, tq), 1)
    bias_newer = jnp.where(w_col > w_row, 0.0, NEG).astype(F32)
    bias_causal = jnp.where(w_col <= w_row, 0.0, NEG).astype(F32)
    clipped = n_w - jnp.minimum(qi, n_w)
    bias_blocks = []
    for b in range(span // tq):
        e = b + clipped
        bias_blocks.append(jnp.where(e == 0, bias_newer,
                                     jnp.where(e == n_w, bias_causal,
                                               jnp.where(e < n_w, 0.0, NEG).astype(F32))))
    bias = jnp.concatenate(bias_blocks, axis=1)
    s_w = s_w + jnp.concatenate([bias] * r_heads, axis=0)
    e_w = jnp.exp2(s_w - jnp.max(s_w, axis=-1, keepdims=True))
    o_win = _dot(e_w.astype(BF16), vw_ref[pl.ds(w_start, span), :]) / jnp.sum(e_w, axis=-1, keepdims=True)
    side_ref[...] = gate_col(0) * o_cmp + gate_col(2) * o_win

    n_blocks = seq // SEL_BLOCK
    n_sel = min(SEL_TOPK, n_blocks)
    m_o = lax.broadcasted_iota(jnp.int32, (LANES, n_cmp), 0)
    n_o = lax.broadcasted_iota(jnp.int32, (LANES, n_cmp), 1)
    overlap_t = jnp.where(n_o * CMP_STRIDE < (m_o + 1) * SEL_BLOCK,
                          jnp.where(n_o * CMP_STRIDE + CMP_BLOCK > m_o * SEL_BLOCK, 1.0, 0.0),
                          0.0).astype(BF16)
    p_hi, p_lo = _split_bf16(p_sum)
    imp = (_dot_nt(overlap_t, p_hi) + _dot_nt(overlap_t, p_lo))[0:n_blocks]
    blk = lax.broadcasted_iota(jnp.int32, (n_blocks, tq), 0)
    t_s = t0 + lax.broadcasted_iota(jnp.int32, (n_blocks, tq), 1)
    cur = t_s // SEL_BLOCK
    forced = (blk == 0) | (blk == cur) | (blk == cur - 1)
    avail = blk * SEL_BLOCK <= t_s
    imp = jnp.where(forced, FORCE, jnp.where(avail, imp, -FORCE))
    sub = 8
    imp_v = [imp[v * sub:(v + 1) * sub] for v in range(n_blocks // sub)]
    blk_v = lax.broadcasted_iota(jnp.int32, (sub, tq), 0)
    rank_v = [jnp.zeros((sub, tq), F32) for _ in imp_v]
    for mp in range(n_blocks):
        other = imp[mp:mp + 1]
        for v in range(len(imp_v)):
            wins_ties = jnp.where(other >= imp_v[v], 1.0, 0.0)
            loses_ties = jnp.where(other > imp_v[v], 1.0, 0.0)
            if v * sub > mp:
                ahead = wins_ties
            elif (v + 1) * sub - 1 <= mp:
                ahead = loses_ties
            else:
                ahead = jnp.where(blk_v + v * sub > mp, wins_ties, loses_ties)
            rank_v[v] = rank_v[v] + ahead
    rank = jnp.concatenate(rank_v, axis=0)
    mask_t = jnp.where(rank < n_sel, 0.0, UNSELECTED)
    mask_t = jnp.concatenate([mask_t, jnp.zeros((LANES - n_blocks, tq), F32)], axis=0)
    mask_feat = mask_t.T.astype(BF16)
    for r in range(r_heads):
        qaug_ref[r * tq:(r + 1) * tq, HEAD_DIM:2 * HEAD_DIM] = mask_feat

    m_ref[...] = jnp.full_like(m_ref, NEG)
    l_ref[...] = jnp.zeros_like(l_ref)
    acc_ref[...] = jnp.zeros_like(acc_ref)
    def scores(j, dst_ref):
        start = pl.multiple_of(j * tk, tk)
        dst_ref[...] = _dot_nt(qaug_ref[...], kaug_ref[pl.ds(start, tk), :])

    def consume(src_ref, j, causal):
        start = pl.multiple_of(j * tk, tk)
        s = src_ref[...]
        if causal:
            kp = start + lax.broadcasted_iota(jnp.int32, (rows, tk), 1)
            s = jnp.where(kp <= t_row, s, NEG)
        m_old = m_ref[...]
        m_new = jnp.maximum(m_old, jnp.max(s, axis=-1, keepdims=True))
        alpha = jnp.exp2(m_old - m_new)
        p = jnp.exp2(s - m_new[:, 0:1])
        l_ref[...] = alpha * l_ref[...] + jnp.sum(p, axis=-1, keepdims=True)
        acc_ref[...] = alpha * acc_ref[...] + _dot(p.astype(BF16), vs_ref[pl.ds(start, tk), :])
        m_ref[...] = m_new

    n_full = t0 // tk
    n_pairs = n_full // 2
    buf_a = sa_ref
    buf_b = sb_ref
    scores(0, buf_a)

    def pair_body(i, carry):
        j = 2 * i
        scores(j + 1, buf_b)
        consume(buf_a, j, False)
        scores(j + 2, buf_a)
        consume(buf_b, j + 1, False)
        return carry

    lax.fori_loop(0, n_pairs, pair_body, 0)
    j_next = 2 * n_pairs

    @pl.when(j_next < n_full)
    def _():
        scores(j_next + 1, buf_b)
        consume(buf_a, j_next, False)
        consume(buf_b, j_next + 1, True)

    @pl.when(j_next == n_full)
    def _():
        consume(buf_a, j_next, True)

    out = side_ref[...] + gate_col(1) * (acc_ref[...] / l_ref[...])
    for r in range(r_heads):
        o_ref[:, r * HEAD_DIM:(r + 1) * HEAD_DIM] = out[r * tq:(r + 1) * tq].astype(o_ref.dtype)


def _nsa_attention(proj, comp, gates, batch, seq):
    t = proj.shape[0]
    tq, tk = NSA_Q_TILE, NSA_K_TILE
    assert seq % tk == 0 and tk % tq == 0 and seq % SEL_BLOCK == 0, (seq, tq, tk)
    assert WINDOW % tq == 0, (WINDOW, tq)
    n_q = seq // tq
    rows = HEADS_PER_GROUP * tq
    n_chunks = comp.shape[3]
    kv_cols = NSA_KV_HEADS
    q_cols = N_HEADS

    def kv_spec(which):
        base = q_cols + which * kv_cols
        return pl.BlockSpec((seq, HEAD_DIM), lambda b, g, qi: (b, base + g))

    def cmp_spec(which):
        return pl.BlockSpec((None, None, None, n_chunks, HEAD_DIM),
                            lambda b, g, qi: (which, b, g, 0, 0))

    return pl.pallas_call(
        _nsa_kernel,
        grid=(batch, NSA_KV_HEADS, n_q),
        in_specs=[
            pl.BlockSpec((tq, HEADS_PER_GROUP * HEAD_DIM), lambda b, g, qi: (b * n_q + qi, g)),
            cmp_spec(0), cmp_spec(1),
            kv_spec(2), kv_spec(3), kv_spec(4), kv_spec(5),
            pl.BlockSpec((tq, LANES), lambda b, g, qi: (b * n_q + qi, g)),
        ],
        out_specs=pl.BlockSpec((tq, HEADS_PER_GROUP * HEAD_DIM), lambda b, g, qi: (b * n_q + qi, g)),
        out_shape=jax.ShapeDtypeStruct((t, N_HEADS * HEAD_DIM), BF16),
        scratch_shapes=[
            pltpu.VMEM((seq, 2 * HEAD_DIM), BF16),
            pltpu.VMEM((rows, 2 * HEAD_DIM), BF16),
            pltpu.VMEM((rows, LANES), F32),
            pltpu.VMEM((rows, LANES), F32),
            pltpu.VMEM((rows, HEAD_DIM), F32),
            pltpu.VMEM((rows, HEAD_DIM), F32),
            pltpu.VMEM((rows, tk), F32),
            pltpu.VMEM((rows, tk), F32),
        ],
        compiler_params=_params(3),
        name="nsa_attention",
    )(proj, comp, comp, proj, proj, proj, proj, gates)


def _sb_kernel(q_ref, k_ref, v_ref, o_ref, carry_ref, acc_ref, za_ref, zb_ref):
    tk = SB_TILE
    qi = pl.program_id(2)
    row = lax.broadcasted_iota(jnp.int32, (tk, tk), 0)
    col = lax.broadcasted_iota(jnp.int32, (tk, tk), 1)
    causal = col < row
    not_before = jnp.where(row >= col, 1.0, 0.0).astype(BF16)
    not_before2 = jnp.concatenate([not_before, not_before], axis=0)
    carry_ref[...] = jnp.zeros_like(carry_ref)
    acc_ref[...] = jnp.zeros_like(acc_ref)
    sign_bit = jnp.uint32(0x80000000)

    heads = range(carry_ref.shape[0])
    hs = [slice(h * HEAD_DIM, (h + 1) * HEAD_DIM) for h in heads]

    def scores(j, dst_ref):
        start = pl.multiple_of(j * tk, tk)
        for h in heads:
            dst_ref[h] = _dot_nt(q_ref[:, hs[h]], k_ref[pl.ds(start, tk), hs[h]])

    def consume(src_ref, j, diagonal):
        start = pl.multiple_of(j * tk, tk)
        z = [src_ref[h] for h in heads]
        neg_abs = [pltpu.bitcast(pltpu.bitcast(z[h], jnp.uint32) | sign_bit, F32) for h in heads]
        sp = [jnp.maximum(z[h], 0.0) + jnp.log(1.0 + jnp.exp2(neg_abs[h])) * LOG2_E for h in heads]
        if diagonal:
            sp = [jnp.where(causal, sp[h], 0.0) for h in heads]
        split = [jnp.concatenate(_split_bf16(sp[h]), axis=1) for h in heads]
        suffix = [_dot(split[h], not_before2) for h in heads]
        carry = [carry_ref[h] for h in heads]
        a = [jnp.exp2(z[h] - suffix[h] - jnp.concatenate([carry[h]] * (tk // LANES), axis=1))
             for h in heads]
        if diagonal:
            a = [jnp.where(causal, a[h], 0.0) for h in heads]
        for h in heads:
            acc_ref[h] += _dot(a[h].astype(BF16), v_ref[pl.ds(start, tk), hs[h]])
            carry_ref[h] = carry[h] + suffix[h][:, 0:1]

    scores(qi, za_ref)
    scores(jnp.maximum(qi - 1, 0), zb_ref)
    consume(za_ref, qi, True)
    n_pairs = jnp.maximum(qi - 1, 0) // 2

    def pair_body(i, c):
        j = qi - 1 - 2 * i
        scores(j - 1, za_ref)
        consume(zb_ref, j, False)
        scores(j - 2, zb_ref)
        consume(za_ref, j - 1, False)
        return c

    lax.fori_loop(0, n_pairs, pair_body, 0)
    j_next = qi - 1 - 2 * n_pairs

    @pl.when(j_next == 1)
    def _():
        scores(0, za_ref)
        consume(zb_ref, 1, False)
        consume(za_ref, 0, False)

    @pl.when(j_next == 0)
    def _():
        consume(zb_ref, 0, False)

    for h in heads:
        o_ref[:, h * HEAD_DIM:(h + 1) * HEAD_DIM] = acc_ref[h].astype(o_ref.dtype)


def _sb_attention(q, kv, batch, seq):
    t = q.shape[0]
    tq = SB_TILE
    n_heads = SB_HEADS
    assert seq % tq == 0 and N_HEADS % n_heads == 0, (seq, tq, n_heads)
    n_q = seq // tq
    width = n_heads * HEAD_DIM
    n_hb = N_HEADS // n_heads
    return pl.pallas_call(
        _sb_kernel,
        grid=(batch, n_hb, n_q),
        in_specs=[
            pl.BlockSpec((tq, width), lambda b, h, qi: (b * n_q + qi, h)),
            pl.BlockSpec((seq, width), lambda b, h, qi: (b, h)),
            pl.BlockSpec((seq, width), lambda b, h, qi: (b, n_hb + h)),
        ],
        out_specs=pl.BlockSpec((tq, width), lambda b, h, qi: (b * n_q + qi, h)),
        out_shape=jax.ShapeDtypeStruct((t, N_HEADS * HEAD_DIM), BF16),
        scratch_shapes=[pltpu.VMEM((n_heads, tq, LANES), F32),
                        pltpu.VMEM((n_heads, tq, HEAD_DIM), F32),
                        pltpu.VMEM((n_heads, tq, SB_TILE), F32),
                        pltpu.VMEM((n_heads, tq, SB_TILE), F32)],
        compiler_params=_params(3),
        name="sb_attention",
    )(q, kv, kv)


def _rope_tables(seq):
    half = HEAD_DIM // 2
    inv = ROPE_THETA ** (-jnp.arange(half, dtype=F32) / half)
    ang = jnp.arange(seq, dtype=F32)[:, None] * inv[None, :]
    cos, sin = jnp.cos(ang), jnp.sin(ang)
    return jnp.concatenate([cos, cos], axis=-1), jnp.concatenate([-sin, sin], axis=-1)


def kernel(x, c, mod_w, mod_b, norm_g, ffn_w_in, ffn_w_out, a_w_in, a_gate_b, a_cmp_pe, a_cmp_w1,
           a_cmp_w2, a_w_out, b_w_q, b_w_out, kv_norm_g, kv_mod_w, kv_mod_b, kv_w):
    batch, seq, d = x.shape
    depth = mod_w.shape[0]
    n_a = a_w_in.shape[0]
    t = batch * seq
    hd = N_HEADS * HEAD_DIM
    kvd = NSA_KV_HEADS * HEAD_DIM
    x2d = x.reshape(t, d)

    c_pad = jnp.pad(c, ((0, 8 - batch), (0, 0)))
    mod = _modulation(c_pad, mod_w, mod_b[:, None, :])[:, :batch]
    mod = mod.reshape(depth, batch, 6, 1, d)
    kv_mod = _modulation(c_pad, kv_mod_w[None], kv_mod_b[None, None, :])[0, :batch]
    kv_mod = kv_mod.reshape(batch, 2, 1, d)
    cos, sin = _rope_tables(seq)
    ffn_w_in_bf16 = ffn_w_in.astype(BF16)
    ffn_w_out_bf16 = ffn_w_out.astype(BF16)

    kv = None
    for layer in range(depth):
        sh1, sc1, g1, sh2, sc2, g2 = (mod[layer, :, i] for i in range(6))
        gam = norm_g[layer][:, None, :]
        if layer < n_a:
            w_in = a_w_in[layer]
            n_main = hd + 6 * kvd
            n_gate = 3 * HEADS_PER_GROUP
            w_gate = w_in[:, n_main:].reshape(d, NSA_KV_HEADS, n_gate)
            w_gate = jnp.pad(w_gate, ((0, 0), (0, 0), (0, LANES - n_gate))).reshape(d, NSA_KV_HEADS * LANES)
            b_gate = jnp.pad(a_gate_b[layer].reshape(NSA_KV_HEADS, n_gate),
                             ((0, 0), (0, LANES - n_gate))).reshape(1, NSA_KV_HEADS * LANES)
            proj, chunks, gates = _project(
                x2d, seq, gam[0], sh1, sc1, w_in[:, :n_main].astype(BF16), mode="nsa",
                extra=(cos, sin, w_gate.astype(BF16), b_gate), tn=2 * kvd)
            comp = _compress(chunks, a_cmp_w1[layer].astype(BF16), a_cmp_w2[layer].astype(BF16),
                             a_cmp_pe[layer].reshape(2, 1, CMP_BLOCK * HEAD_DIM))
            o = _nsa_attention(proj, comp, gates, batch, seq)
            x2d = _out_project(o, a_w_out[layer].astype(BF16), x2d, seq, g1, gam[1])
        else:
            jb = layer - n_a
            if kv is None:
                kv = _project(x2d, seq, kv_norm_g[None, :], kv_mod[:, 0], kv_mod[:, 1],
                              kv_w.astype(BF16))
            q = _project(x2d, seq, gam[0], sh1, sc1, b_w_q[jb].astype(BF16), out_scale=Q_SCALE)
            o = _sb_attention(q, kv, batch, seq)
            x2d = _out_project(o, b_w_out[jb].astype(BF16), x2d, seq, g1, gam[1])
        x2d = _ffn(x2d, seq, gam[2], sh2, sc2, ffn_w_in_bf16, ffn_w_out_bf16, layer, g2, gam[3])
    return x2d.reshape(batch, seq, d)
```
